```python
import math
import jax, jax.numpy as jnp
from jax import lax
import numpy as np

D_MODEL = 1024
BATCH = 16
SEQ = 4096
DEPTH = 1
DEC_BATCH = 128
DEC_SEQ = 8
PAST_LEN = 8192
PAGE_SIZE = 128

N_HEADS = 8
HEAD_DIM = 64
D_ATTN = N_HEADS * HEAD_DIM
MOBA_BLOCK = 256
MOBA_TOPK = 3
Q_CHUNK = 128
D_LRU = 512
LRU_BLOCKS = 8
LRU_BLOCK = D_LRU // LRU_BLOCKS
CONV_W = 4
LRU_C = 8.0
PEER_HEADS = 8
PEER_NKEYS = 128
PEER_EXPERTS = PEER_NKEYS * PEER_NKEYS
PEER_QDIM = 256
PEER_HALF = PEER_QDIM // 2
PEER_TOPK = 16
PEER_CHUNK = 256
D_IN = 3 * D_ATTN + 2 * D_LRU + 2 * D_MODEL
SPLITS = (D_ATTN, 2 * D_ATTN, 3 * D_ATTN, 3 * D_ATTN + D_LRU, 3 * D_ATTN + 2 * D_LRU, 3 * D_ATTN + 2 * D_LRU + D_MODEL)
EPS = 1e-6

kernel_name = 'moba_rglru_peer_hybrid_step'


def rms_norm(x, g):
    xf = x.astype(jnp.float32)
    y = xf * lax.rsqrt(jnp.mean(xf * xf, axis=-1, keepdims=True) + EPS)
    return (y * g.astype(jnp.float32)).astype(x.dtype)


def alibi_slopes():
    return jnp.asarray(np.array([2.0 ** (-8.0 * (h + 1) / N_HEADS) for h in range(N_HEADS)], dtype=np.float32))


def moba_sequence(q, k, v, q_pos0):
    f32 = jnp.float32
    tq = q.shape[0]
    tk = k.shape[0]
    nb = -(-tk // MOBA_BLOCK)
    pad = nb * MOBA_BLOCK - tk
    kb = jnp.pad(k, ((0, pad), (0, 0), (0, 0))).reshape(nb, MOBA_BLOCK, N_HEADS, HEAD_DIM)
    vb = jnp.pad(v, ((0, pad), (0, 0), (0, 0))).reshape(nb, MOBA_BLOCK, N_HEADS, HEAD_DIM)
    qpos = q_pos0 + jnp.arange(tq)
    qblk = qpos // MOBA_BLOCK
    n_sel = min(MOBA_TOPK, (q_pos0 + tq - 1) // MOBA_BLOCK)
    slopes = alibi_slopes()
    scale = HEAD_DIM ** -0.5
    chunk = min(Q_CHUNK, tq)
    n_chunks = tq // chunk
    hidx = jnp.arange(N_HEADS)[None, :, None]
    if n_sel > 0:
        kmean = jnp.mean(kb.astype(f32), axis=1)
        gate = jnp.einsum('thd,nhd->thn', q.astype(f32), kmean)
        fully_past = jnp.arange(nb)[None, None, :] < qblk[:, None, None]
        _, sel = lax.top_k(jnp.where(fully_past, gate, -jnp.inf), n_sel)
        sel_ok = sel < qblk[:, None, None]
        kbh = jnp.swapaxes(kb, 1, 2)
        vbh = jnp.swapaxes(vb, 1, 2)

    def one_chunk(ci):
        c0 = ci * chunk
        qc = lax.dynamic_slice_in_dim(q, c0, chunk, 0).astype(f32)
        pc = q_pos0 + c0 + jnp.arange(chunk)
        ob = (q_pos0 + c0) // MOBA_BLOCK
        ko = lax.dynamic_index_in_dim(kb, ob, 0, keepdims=False).astype(f32)
        vo = lax.dynamic_index_in_dim(vb, ob, 0, keepdims=False).astype(f32)
        opos = ob * MOBA_BLOCK + jnp.arange(MOBA_BLOCK)
        dist = (pc[:, None] - opos[None, :]).astype(f32)
        s_own = jnp.einsum('chd,shd->chs', qc, ko) * scale - slopes[None, :, None] * dist[:, None, :]
        s_own = jnp.where((dist >= 0)[:, None, :], s_own, -jnp.inf)
        if n_sel == 0:
            p = jax.nn.softmax(s_own, axis=-1)
            out = jnp.einsum('chs,shd->chd', p, vo)
        else:
            sc = lax.dynamic_slice_in_dim(sel, c0, chunk, 0)
            okc = lax.dynamic_slice_in_dim(sel_ok, c0, chunk, 0)
            ks = kbh[sc, hidx].astype(f32)
            vs = vbh[sc, hidx].astype(f32)
            spos = sc[..., None] * MOBA_BLOCK + jnp.arange(MOBA_BLOCK)
            sdist = (pc[:, None, None, None] - spos).astype(f32)
            s_sel = jnp.einsum('chd,chnsd->chns', qc, ks) * scale - slopes[None, :, None, None] * sdist
            s_sel = jnp.where(okc[..., None], s_sel, -jnp.inf).reshape(chunk, N_HEADS, n_sel * MOBA_BLOCK)
            p = jax.nn.softmax(jnp.concatenate([s_sel, s_own], axis=-1), axis=-1)
            p_sel = p[..., :n_sel * MOBA_BLOCK].reshape(chunk, N_HEADS, n_sel, MOBA_BLOCK)
            out = (jnp.einsum('chns,chnsd->chd', p_sel, vs)
                   + jnp.einsum('chs,shd->chd', p[..., n_sel * MOBA_BLOCK:], vo))
        return out.astype(q.dtype)

    return lax.map(one_chunk, jnp.arange(n_chunks)).reshape(tq, N_HEADS, HEAD_DIM)


def rg_lru(x, h0, w_a, b_a, w_x, b_x, lam):
    f32 = jnp.float32
    bsz, t, _ = x.shape
    xb = x.reshape(bsz, t, LRU_BLOCKS, LRU_BLOCK)
    r = jax.nn.sigmoid(jnp.einsum('btni,nij->btnj', xb, w_a).reshape(bsz, t, D_LRU) + b_a).astype(f32)
    i = jax.nn.sigmoid(jnp.einsum('btni,nij->btnj', xb, w_x).reshape(bsz, t, D_LRU) + b_x).astype(f32)
    log_a = -LRU_C * r * jax.nn.softplus(-lam.astype(f32))
    a = jnp.exp(log_a)
    b = jnp.sqrt(-jnp.expm1(2.0 * log_a)) * (i * x.astype(f32))

    def combine(c1, c2):
        a1, b1 = c1
        a2, b2 = c2
        return a1 * a2, a2 * b1 + b2

    a_cum, b_cum = lax.associative_scan(combine, (a, b), axis=1)
    h = a_cum * h0.astype(f32)[:, None, :] + b_cum
    return h.astype(x.dtype), h[:, -1].astype(h0.dtype)


def peer(xn, w_pq, sub_keys, peer_u, peer_v):
    f32 = jnp.float32
    shp = xn.shape
    xt = xn.reshape(-1, D_MODEL)
    n = xt.shape[0]
    chunk = min(PEER_CHUNK, n)
    n_pad = -(-n // chunk) * chunk
    xt = jnp.pad(xt, ((0, n_pad - n), (0, 0))).reshape(n_pad // chunk, chunk, D_MODEL)

    def one(xc):
        qh = jnp.einsum('cd,de->ce', xc, w_pq).reshape(chunk, PEER_HEADS, 2, PEER_HALF).astype(f32)
        s = jnp.einsum('chpk,hpnk->chpn', qh, sub_keys.astype(f32))
        sv, si = lax.top_k(s, PEER_TOPK)
        cand = (sv[:, :, 0, :, None] + sv[:, :, 1, None, :]).reshape(chunk, PEER_HEADS, PEER_TOPK * PEER_TOPK)
        cid = (si[:, :, 0, :, None] * PEER_NKEYS + si[:, :, 1, None, :]).reshape(chunk, PEER_HEADS, PEER_TOPK * PEER_TOPK)
        fv, fi = lax.top_k(cand, PEER_TOPK)
        eid = jnp.take_along_axis(cid, fi, axis=-1)
        g = jax.nn.softmax(fv, axis=-1)
        act = jax.nn.gelu(jnp.einsum('chkd,cd->chk', peer_u[eid], xc).astype(f32))
        wgt = (g * act).astype(xc.dtype)
        return jnp.einsum('chk,chkd->cd', wgt, peer_v[eid])

    out = lax.map(one, xt).reshape(n_pad, D_MODEL)[:n]
    return out.reshape(shp)


def hybrid_layer(x, conv_buf, h0, attend, w):
    (g_mix, w_in, b_in, g_q, g_k, w_conv, b_conv, w_a, b_a, w_x, b_x, lam,
     w_oa, w_ob, w_out, g_ffn, w_pq, sub_keys, peer_u, peer_v) = w
    bsz, t, _ = x.shape
    xn = rms_norm(x, g_mix)
    proj = jnp.einsum('btd,de->bte', xn, w_in) + b_in
    q, k, v, xr, yr, ga, gb = jnp.split(proj, SPLITS, axis=-1)
    q = rms_norm(q.reshape(bsz, t, N_HEADS, HEAD_DIM), g_q)
    k = rms_norm(k.reshape(bsz, t, N_HEADS, HEAD_DIM), g_k)
    v = v.reshape(bsz, t, N_HEADS, HEAD_DIM)
    attn = attend(q, k, v).reshape(bsz, t, D_ATTN)
    xp = jnp.concatenate([conv_buf.astype(xr.dtype), xr], axis=1)
    xc = b_conv + w_conv[0] * xp[:, 0:t]
    for j in range(1, CONV_W):
        xc = xc + w_conv[j] * xp[:, j:j + t]
    new_conv = xp[:, t:]
    hseq, h_last = rg_lru(xc, h0, w_a, b_a, w_x, b_x, lam)
    rec = hseq * jax.nn.gelu(yr)
    merged = (jax.nn.sigmoid(ga) * jnp.einsum('bte,ed->btd', attn, w_oa)
              + jax.nn.sigmoid(gb) * jnp.einsum('bte,ed->btd', rec, w_ob))
    x = x + jnp.einsum('btd,de->bte', merged, w_out)
    x = x + peer(rms_norm(x, g_ffn), w_pq, sub_keys, peer_u, peer_v)
    return x, k, v, h_last, new_conv


def setup_inputs(seed: int = 0) -> dict:
    key = jax.random.key(seed)
    ks = jax.random.split(key, 32)
    f32 = jnp.float32
    n_pages = PAST_LEN // PAGE_SIZE
    n_pool = (5 * DEC_BATCH * n_pages + 3) // 4

    def nrm(k, shape, s):
        return jax.random.normal(k, shape, f32) * s

    a_c = jax.random.uniform(ks[14], (D_LRU,), f32, 0.9, 0.999)
    base = a_c ** (1.0 / LRU_C)
    lam = jnp.log(base) - jnp.log1p(-base)
    page_table = jax.random.permutation(ks[6], n_pool)[: DEC_BATCH * n_pages].reshape(DEC_BATCH, n_pages).astype(jnp.int32)
    return {
        'x_prompt': nrm(ks[0], (BATCH, SEQ, D_MODEL), 1.0),
        'x_sample': nrm(ks[1], (DEC_BATCH, DEC_SEQ, D_MODEL), 1.0),
        'cache_k': nrm(ks[2], (n_pool, PAGE_SIZE, N_HEADS, HEAD_DIM), 1.0),
        'cache_v': nrm(ks[3], (n_pool, PAGE_SIZE, N_HEADS, HEAD_DIM), 1.0),
        'state_h': nrm(ks[4], (DEC_BATCH, D_LRU), 0.5),
        'state_conv': nrm(ks[5], (DEC_BATCH, CONV_W - 1, D_LRU), 1.0),
        'page_table': page_table,
        'g_mix': 1.0 + nrm(ks[7], (D_MODEL,), 0.05),
        'w_in': nrm(ks[8], (D_MODEL, D_IN), D_MODEL ** -0.5),
        'b_in': nrm(ks[9], (D_IN,), 0.02),
        'g_q': 1.0 + nrm(ks[10], (HEAD_DIM,), 0.05),
        'g_k': 1.0 + nrm(ks[11], (HEAD_DIM,), 0.05),
        'w_conv': nrm(ks[12], (CONV_W, D_LRU), CONV_W ** -0.5),
        'b_conv': nrm(ks[13], (D_LRU,), 0.02),
        'w_a': nrm(ks[15], (LRU_BLOCKS, LRU_BLOCK, LRU_BLOCK), LRU_BLOCK ** -0.5),
        'b_a': nrm(ks[16], (D_LRU,), 0.02),
        'w_x': nrm(ks[17], (LRU_BLOCKS, LRU_BLOCK, LRU_BLOCK), LRU_BLOCK ** -0.5),
        'b_x': nrm(ks[18], (D_LRU,), 0.02),
        'lam': lam,
        'w_oa': nrm(ks[19], (D_ATTN, D_MODEL), D_ATTN ** -0.5),
        'w_ob': nrm(ks[20], (D_LRU, D_MODEL), D_LRU ** -0.5),
        'w_out': nrm(ks[21], (D_MODEL, D_MODEL), D_MODEL ** -0.5),
        'g_ffn': 1.0 + nrm(ks[22], (D_MODEL,), 0.05),
        'w_pq': nrm(ks[23], (D_MODEL, PEER_HEADS * PEER_QDIM), D_MODEL ** -0.5),
        'sub_keys': nrm(ks[24], (PEER_HEADS, 2, PEER_NKEYS, PEER_HALF), PEER_HALF ** -0.5),
        'peer_u': nrm(ks[25], (PEER_EXPERTS, D_MODEL), D_MODEL ** -0.5),
        'peer_v': nrm(ks[26], (PEER_EXPERTS, D_MODEL), PEER_HEADS ** -0.5),
    }


def reference(x_prompt, x_sample, cache_k, cache_v, state_h, state_conv, page_table,
              g_mix, w_in, b_in, g_q, g_k, w_conv, b_conv, w_a, b_a, w_x, b_x, lam,
              w_oa, w_ob, w_out, g_ffn, w_pq, sub_keys, peer_u, peer_v):
    weights = (g_mix, w_in, b_in, g_q, g_k, w_conv, b_conv, w_a, b_a, w_x, b_x, lam,
               w_oa, w_ob, w_out, g_ffn, w_pq, sub_keys, peer_u, peer_v)
    past = page_table.shape[1] * cache_k.shape[1]

    def attend_prompt(q, k, v):
        return lax.map(lambda a: moba_sequence(a[0], a[1], a[2], 0), (q, k, v))

    def attend_sample(q, k, v):
        def one(a):
            qs, kn, vn, pt = a
            kp = cache_k[pt].reshape(past, N_HEADS, HEAD_DIM)
            vp = cache_v[pt].reshape(past, N_HEADS, HEAD_DIM)
            k_all = jnp.concatenate([kp, kn.astype(kp.dtype)], axis=0)
            v_all = jnp.concatenate([vp, vn.astype(vp.dtype)], axis=0)
            return moba_sequence(qs, k_all, v_all, past)
        return lax.map(one, (q, k, v, page_table))

    conv0 = jnp.zeros((x_prompt.shape[0], CONV_W - 1, D_LRU), x_prompt.dtype)
    h00 = jnp.zeros((x_prompt.shape[0], D_LRU), state_h.dtype)
    y_prompt, y_sample = x_prompt, x_sample
    for _ in range(DEPTH):
        y_prompt, k_prompt, v_prompt, h_prompt, conv_prompt = hybrid_layer(y_prompt, conv0, h00, attend_prompt, weights)
        y_sample, k_sample, v_sample, h_sample, conv_sample = hybrid_layer(y_sample, state_conv, state_h, attend_sample, weights)
    return (y_prompt, y_sample, k_prompt, v_prompt, h_prompt, conv_prompt, k_sample, v_sample, h_sample, conv_sample)
```

```python
import functools
import math

import jax
import jax.numpy as jnp
import numpy as np
from jax import lax
from jax.experimental import pallas as pl
from jax.experimental.pallas import tpu as pltpu

F32 = jnp.float32
BF16 = jnp.bfloat16
I32 = jnp.int32

D_MODEL = 1024
N_HEADS = 8
HEAD_DIM = 64
D_ATTN = N_HEADS * HEAD_DIM
MOBA_BLOCK = 256
MOBA_TOPK = 3
PAGE_SIZE = 128
D_LRU = 512
LRU_BLOCKS = 8
CONV_W = 4
LRU_C = 8.0
PEER_HEADS = 8
PEER_NKEYS = 128
PEER_HALF = 128
PEER_TOPK = 16
PEER_SEL = PEER_HEADS * PEER_TOPK
EPS = 1e-6
NEG_INF = float("-inf")

LANES = 128
HALF_ROW_WORDS = D_MODEL // 2
SLAB = HALF_ROW_WORDS // LANES
VMEM_LIMIT = 56 * 1024 * 1024


def _cparams(sem):
    return pltpu.CompilerParams(dimension_semantics=sem, vmem_limit_bytes=VMEM_LIMIT)


def _gelu(x):
    c = math.sqrt(2.0 / math.pi)
    return x * (0.5 * (1.0 + jnp.tanh(c * (x + 0.044715 * (x * x * x)))))


def _dot(a, b):
    return jnp.dot(a.astype(BF16), b.astype(BF16), preferred_element_type=F32)


def _dot_nt(a, b, precision=None):
    return lax.dot_general(a, b, (((1,), (1,)), ((), ())), precision=precision,
                           preferred_element_type=F32)


def _inproj_kernel(x_ref, gmix_ref, w_ref, b_ref, gq_ref, gk_ref, ones_ref,
                   q_ref, k_ref, v_ref, xr_ref, yr_ref, ga_ref, gb_ref):
    x = x_ref[...]
    ms = jnp.mean(x * x, axis=-1, keepdims=True)
    xn = x * lax.rsqrt(ms + EPS) * gmix_ref[...]
    proj = _dot(xn, w_ref[...]) + b_ref[...]

    def head_norm(t, g):
        t2 = t * t
        hi = t2.astype(BF16)
        lo = (t2 - hi.astype(F32)).astype(BF16)
        s = (jnp.dot(hi, ones_ref[...], preferred_element_type=F32)
             + jnp.dot(lo, ones_ref[...], preferred_element_type=F32))
        return t * lax.rsqrt(s * (1.0 / HEAD_DIM) + EPS) * g

    a = D_ATTN
    q_ref[...] = head_norm(proj[:, 0:a], gq_ref[...])
    k_ref[...] = head_norm(proj[:, a:2 * a], gk_ref[...])
    v_ref[...] = proj[:, 2 * a:3 * a]
    xr_ref[...] = proj[:, 3 * a:3 * a + D_LRU]
    yr_ref[...] = proj[:, 3 * a + D_LRU:3 * a + 2 * D_LRU]
    o = 3 * a + 2 * D_LRU
    ga_ref[...] = proj[:, o:o + D_MODEL]
    gb_ref[...] = proj[:, o + D_MODEL:o + 2 * D_MODEL]


def _in_proj(x, gmix, w_in, b_in, gq, gk, ones_bd, tm):
    n = x.shape[0]
    d_in = w_in.shape[1]
    row = lambda i: (i, 0)
    fix = lambda i: (0, 0)
    outs = [D_ATTN, D_ATTN, D_ATTN, D_LRU, D_LRU, D_MODEL, D_MODEL]
    return pl.pallas_call(
        _inproj_kernel,
        grid=(n // tm,),
        in_specs=[pl.BlockSpec((tm, D_MODEL), row),
                  pl.BlockSpec((1, D_MODEL), fix),
                  pl.BlockSpec((D_MODEL, d_in), fix),
                  pl.BlockSpec((1, d_in), fix),
                  pl.BlockSpec((1, D_ATTN), fix),
                  pl.BlockSpec((1, D_ATTN), fix),
                  pl.BlockSpec((D_ATTN, D_ATTN), fix)],
        out_specs=[pl.BlockSpec((tm, w), row) for w in outs],
        out_shape=[jax.ShapeDtypeStruct((n, w), F32) for w in outs],
        compiler_params=_cparams(("parallel",)),
        name="in_proj",
    )(x, gmix, w_in, b_in, gq, gk, ones_bd)


def _moba_prompt_kernel(slopes_ref, q_ref, k_ref, v_ref, o_ref,
                        kmean_ref, sel_ref, m_ref, l_ref, acc_ref, *, nblk):
    hp = pl.program_id(1)
    i = pl.program_id(2)
    blk = MOBA_BLOCK

    @pl.when(i == 0)
    def _():
        for j in range(nblk):
            kmean_ref[j:j + 1, :] = jnp.mean(k_ref[j * blk:(j + 1) * blk, :], axis=0, keepdims=True)

    q2 = q_ref[...]
    lane = lax.broadcasted_iota(I32, (blk, LANES), 1)
    rc = (lax.broadcasted_iota(I32, (blk, blk), 0)
          - lax.broadcasted_iota(I32, (blk, blk), 1)).astype(F32)
    jl = lax.broadcasted_iota(I32, (blk, nblk), 1)
    own0 = pl.multiple_of(i * blk, blk)
    heads = []
    for hh in range(2):
        slope = slopes_ref[hp * 2 + hh]
        qh = jnp.where((lane >= hh * HEAD_DIM) & (lane < (hh + 1) * HEAD_DIM), q2, 0.0)
        gate = _dot_nt(qh, kmean_ref[...], precision=lax.Precision.HIGHEST)
        cnt = jnp.zeros((blk, nblk), I32)
        for ii in range(nblk):
            gi = gate[:, ii:ii + 1]
            better = (gi > gate) | ((gi == gate) & (ii < jl))
            cnt = cnt + jnp.where(better & (ii < i), 1, 0)
        sel_ref[...] = jnp.where((cnt < MOBA_TOPK) & (jl < i), 1.0, 0.0)

        qs = (qh * (HEAD_DIM ** -0.5)).astype(BF16)
        s = _dot_nt(qs, k_ref[pl.ds(own0, blk), :].astype(BF16)) - slope * rc
        s = jnp.where(rc >= 0.0, s, NEG_INF)
        m0 = jnp.max(s, axis=-1, keepdims=True)
        p = jnp.exp(s - m0)
        m_ref[...] = m0
        l_ref[...] = jnp.sum(p, axis=-1, keepdims=True)
        acc_ref[...] = _dot(p, v_ref[pl.ds(own0, blk), :])

        def body(j, carry):
            k0 = pl.multiple_of(j * blk, blk)
            dist = rc + ((i - j) * blk).astype(F32)
            sj = _dot_nt(qs, k_ref[pl.ds(k0, blk), :].astype(BF16)) - slope * dist
            selj = jnp.sum(jnp.where(jl == j, sel_ref[...], 0.0), axis=-1, keepdims=True)
            sj = jnp.where(selj > 0.0, sj, NEG_INF)
            m_old = m_ref[...]
            m_new = jnp.maximum(m_old, jnp.max(sj, axis=-1, keepdims=True))
            alpha = jnp.exp(m_old - m_new)
            pj = jnp.exp(sj - m_new)
            l_ref[...] = alpha * l_ref[...] + jnp.sum(pj, axis=-1, keepdims=True)
            acc_ref[...] = alpha * acc_ref[...] + _dot(pj, v_ref[pl.ds(k0, blk), :])
            m_ref[...] = m_new
            return carry

        lax.fori_loop(0, i, body, 0)
        heads.append(acc_ref[...] / l_ref[...])
    o_ref[...] = jnp.where(lane < HEAD_DIM, heads[0], heads[1])


def _moba_prompt(q, k, v, slopes, bsz, t):
    nblk = t // MOBA_BLOCK
    blk = MOBA_BLOCK
    kern = functools.partial(_moba_prompt_kernel, nblk=nblk)
    return pl.pallas_call(
        kern,
        grid_spec=pltpu.PrefetchScalarGridSpec(
            num_scalar_prefetch=1,
            grid=(bsz, N_HEADS // 2, nblk),
            in_specs=[pl.BlockSpec((blk, LANES), lambda b, hp, i, s: (b * nblk + i, hp)),
                      pl.BlockSpec((t, LANES), lambda b, hp, i, s: (b, hp)),
                      pl.BlockSpec((t, LANES), lambda b, hp, i, s: (b, hp))],
            out_specs=pl.BlockSpec((blk, LANES), lambda b, hp, i, s: (b * nblk + i, hp)),
            scratch_shapes=[pltpu.VMEM((nblk, LANES), F32),
                            pltpu.VMEM((blk, nblk), F32),
                            pltpu.VMEM((blk, 1), F32),
                            pltpu.VMEM((blk, 1), F32),
                            pltpu.VMEM((blk, LANES), F32)]),
        out_shape=jax.ShapeDtypeStruct((bsz * t, D_ATTN), F32),
        compiler_params=_cparams(("parallel", "parallel", "arbitrary")),
        name="moba_prompt",
    )(slopes, q, k, v)


def _moba_sample_kernel(pt_ref, slopes_ref, q_ref, kn_ref, vn_ref, k0_ref, k1_ref, v0_ref, v1_ref,
                        o_ref, qb_ref, m_ref, l_ref, g_ref, oall_ref, *, nblk, tq, past):
    j = pl.program_id(1)
    rows = N_HEADS * tq
    blk = MOBA_BLOCK
    lane_d = lax.broadcasted_iota(I32, (rows, D_ATTN), 1)
    row_d = lax.broadcasted_iota(I32, (rows, D_ATTN), 0)
    lane_s = lax.broadcasted_iota(I32, (rows, LANES), 1)

    def row_consts(width):
        r = lax.broadcasted_iota(I32, (rows, width), 0)
        hrow = r // tq
        slope = jnp.zeros((rows, width), F32)
        for h in range(N_HEADS):
            slope = jnp.where(hrow == h, slopes_ref[h], slope)
        qpos = (past + (r - hrow * tq)).astype(F32)
        return slope, qpos

    @pl.when(j == 0)
    def _():
        qt = jnp.concatenate([q_ref[...]] * N_HEADS, axis=0)
        qb_ref[...] = jnp.where(lane_d // HEAD_DIM == row_d // tq, qt, 0.0)
        m_ref[...] = jnp.zeros_like(m_ref)
        l_ref[...] = jnp.zeros_like(l_ref)
        g_ref[...] = jnp.zeros_like(g_ref)

    qb = qb_ref[...]
    qs = (qb * (HEAD_DIM ** -0.5)).astype(BF16)

    @pl.when(j < nblk)
    def _():
        kb = jnp.concatenate([k0_ref[0], k1_ref[0]], axis=0)
        vb = jnp.concatenate([v0_ref[0], v1_ref[0]], axis=0)
        slope, qpos = row_consts(blk)
        kpos = (j * blk + lax.broadcasted_iota(I32, (rows, blk), 1)).astype(F32)
        s = _dot_nt(qs, kb.astype(BF16)) - slope * (qpos - kpos)
        mj = jnp.max(s, axis=-1, keepdims=True)
        p = jnp.exp(s - mj)
        lj = jnp.sum(p, axis=-1, keepdims=True)
        oall_ref[j] = _dot(p, vb)
        kmean = jnp.mean(kb, axis=0, keepdims=True)
        gj = jnp.sum(qb * kmean, axis=-1, keepdims=True)
        here = lane_s == j
        m_ref[...] = jnp.where(here, mj, m_ref[...])
        l_ref[...] = jnp.where(here, lj, l_ref[...])
        g_ref[...] = jnp.where(here, gj, g_ref[...])

    @pl.when(j == nblk)
    def _():
        slope, qpos = row_consts(tq)
        kpos = (past + lax.broadcasted_iota(I32, (rows, tq), 1)).astype(F32)
        dist = qpos - kpos
        s = _dot_nt(qs, kn_ref[...].astype(BF16)) - slope * dist
        s = jnp.where(dist >= 0.0, s, NEG_INF)
        m_o = jnp.max(s, axis=-1, keepdims=True)
        p = jnp.exp(s - m_o)
        l_o = jnp.sum(p, axis=-1, keepdims=True)
        o_o = _dot(p, vn_ref[...])
        gate = g_ref[...]
        cnt = jnp.zeros((rows, LANES), I32)
        for ii in range(nblk):
            gi = gate[:, ii:ii + 1]
            better = (gi > gate) | ((gi == gate) & (ii < lane_s))
            cnt = cnt + jnp.where(better, 1, 0)
        sel = (cnt < MOBA_TOPK) & (lane_s < nblk)
        m_all = m_ref[...]
        m_tot = jnp.maximum(m_o, jnp.max(jnp.where(sel, m_all, NEG_INF), axis=-1, keepdims=True))
        w = jnp.where(sel, jnp.exp(m_all - m_tot), 0.0)
        w_o = jnp.exp(m_o - m_tot)
        den = jnp.sum(w * l_ref[...], axis=-1, keepdims=True) + w_o * l_o
        num = w_o * o_o
        for jj in range(nblk):
            num = num + w[:, jj:jj + 1] * oall_ref[jj]
        res = num / den
        res = jnp.where(lane_d // HEAD_DIM == row_d // tq, res, 0.0)
        out = res[0:tq]
        for h in range(1, N_HEADS):
            out = out + res[h * tq:(h + 1) * tq]
        o_ref[...] = out


def _moba_sample(q, kn, vn, cache_k, cache_v, page_table, slopes, bsz, tq):
    n_pages = page_table.shape[1]
    per = MOBA_BLOCK // PAGE_SIZE
    nblk = n_pages // per
    past = n_pages * PAGE_SIZE
    rows = N_HEADS * tq
    n_pool = cache_k.shape[0]
    ck = cache_k.reshape(n_pool, PAGE_SIZE, D_ATTN)
    cv = cache_v.reshape(n_pool, PAGE_SIZE, D_ATTN)
    new = pl.BlockSpec((tq, D_ATTN), lambda b, j, pt, s: (b, 0))

    def page(which):
        return pl.BlockSpec(
            (1, PAGE_SIZE, D_ATTN),
            lambda b, j, pt, s: (pt[b, per * jnp.minimum(j, nblk - 1) + which], 0, 0))

    kern = functools.partial(_moba_sample_kernel, nblk=nblk, tq=tq, past=past)
    return pl.pallas_call(
        kern,
        grid_spec=pltpu.PrefetchScalarGridSpec(
            num_scalar_prefetch=2,
            grid=(bsz, nblk + 1),
            in_specs=[new, new, new, page(0), page(1), page(0), page(1)],
            out_specs=pl.BlockSpec((tq, D_ATTN), lambda b, j, pt, s: (b, 0)),
            scratch_shapes=[pltpu.VMEM((rows, D_ATTN), F32),
                            pltpu.VMEM((rows, LANES), F32),
                            pltpu.VMEM((rows, LANES), F32),
                            pltpu.VMEM((rows, LANES), F32),
                            pltpu.VMEM((nblk, rows, D_ATTN), F32)]),
        out_shape=jax.ShapeDtypeStruct((bsz * tq, D_ATTN), F32),
        compiler_params=_cparams(("parallel", "arbitrary")),
        name="moba_sample",
    )(page_table, slopes, q, kn, vn, ck, ck, cv, cv)


def _rglru_kernel(xr_ref, yr_ref, conv0_ref, h0_ref, wconv_ref, bconv_ref, wa_ref, ba_ref,
                  wx_ref, bx_ref, lam_ref, rec_ref, hlast_ref, convout_ref,
                  xpad_ref, a_ref, b_ref, hs_ref, h_ref, *, tc):
    c = pl.program_id(1)
    nc = pl.num_programs(1)
    pad = 8
    keep = CONV_W - 1

    @pl.when(c == 0)
    def _():
        xpad_ref[pad - keep:pad, :] = conv0_ref[0]
        h_ref[...] = h0_ref[0]

    xpad_ref[pad:pad + tc, :] = xr_ref[...]
    xc = bconv_ref[...] + wconv_ref[0:1, :] * xpad_ref[pad - 3:pad - 3 + tc, :]
    for jj in range(1, CONV_W):
        xc = xc + wconv_ref[jj:jj + 1, :] * xpad_ref[pad - 3 + jj:pad - 3 + jj + tc, :]
    tail = xpad_ref[pad + tc - keep:pad + tc, :]
    xpad_ref[pad - keep:pad, :] = tail
    convout_ref[0] = tail

    r = jax.nn.sigmoid(_dot(xc, wa_ref[...]) + ba_ref[...])
    gi = jax.nn.sigmoid(_dot(xc, wx_ref[...]) + bx_ref[...])
    lam = lam_ref[...]
    softplus_neg = jnp.maximum(-lam, 0.0) + jnp.log1p(jnp.exp(-jnp.abs(lam)))
    log_a = (-LRU_C) * r * softplus_neg
    a_ref[...] = jnp.exp(log_a)
    th = jnp.tanh(log_a)
    b_ref[...] = jnp.sqrt(-2.0 * th / (1.0 - th)) * (gi * xc)

    def step(t8, h):
        t0 = pl.multiple_of(t8 * 8, 8)
        a8 = a_ref[pl.ds(t0, 8), :]
        b8 = b_ref[pl.ds(t0, 8), :]
        out = []
        for rr in range(8):
            h = a8[rr:rr + 1, :] * h + b8[rr:rr + 1, :]
            out.append(h)
        hs_ref[pl.ds(t0, 8), :] = jnp.concatenate(out, axis=0)
        return h

    h = lax.fori_loop(0, tc // 8, step, h_ref[...])
    h_ref[...] = h
    rec_ref[...] = hs_ref[...] * _gelu(yr_ref[...])

    @pl.when(c == nc - 1)
    def _():
        hlast_ref[0] = h


def _rglru(xr, yr, conv0, h0, w_conv, b_conv, wa_bd, b_a, wx_bd, b_x, lam, bsz, t, tc):
    ncs = t // tc
    row = lambda b, c: (b * ncs + c, 0)
    per_b = lambda b, c: (b, 0, 0)
    fix = lambda b, c: (0, 0)
    kern = functools.partial(_rglru_kernel, tc=tc)
    vec = pl.BlockSpec((1, D_LRU), fix)
    return pl.pallas_call(
        kern,
        grid=(bsz, ncs),
        in_specs=[pl.BlockSpec((tc, D_LRU), row),
                  pl.BlockSpec((tc, D_LRU), row),
                  pl.BlockSpec((1, CONV_W - 1, D_LRU), per_b),
                  pl.BlockSpec((1, 1, D_LRU), per_b),
                  pl.BlockSpec((CONV_W, D_LRU), fix),
                  vec,
                  pl.BlockSpec((D_LRU, D_LRU), fix), vec,
                  pl.BlockSpec((D_LRU, D_LRU), fix), vec,
                  vec],
        out_specs=[pl.BlockSpec((tc, D_LRU), row),
                   pl.BlockSpec((1, 1, D_LRU), per_b),
                   pl.BlockSpec((1, CONV_W - 1, D_LRU), per_b)],
        out_shape=[jax.ShapeDtypeStruct((bsz * t, D_LRU), F32),
                   jax.ShapeDtypeStruct((bsz, 1, D_LRU), F32),
                   jax.ShapeDtypeStruct((bsz, CONV_W - 1, D_LRU), F32)],
        scratch_shapes=[pltpu.VMEM((tc + 8, D_LRU), F32),
                        pltpu.VMEM((tc, D_LRU), F32),
                        pltpu.VMEM((tc, D_LRU), F32),
                        pltpu.VMEM((tc, D_LRU), F32),
                        pltpu.VMEM((1, D_LRU), F32)],
        compiler_params=_cparams(("parallel", "arbitrary")),
        name="rglru",
    )(xr, yr, conv0, h0, w_conv, b_conv, wa_bd, b_a, wx_bd, b_x, lam)


def _merge_kernel(x_ref, attn_ref, rec_ref, ga_ref, gb_ref, woa_ref, wob_ref, wout_ref, gffn_ref,
                  x2_ref, xn_ref):
    merged = (jax.nn.sigmoid(ga_ref[...]) * _dot(attn_ref[...], woa_ref[...])
              + jax.nn.sigmoid(gb_ref[...]) * _dot(rec_ref[...], wob_ref[...]))
    x2 = x_ref[...] + _dot(merged, wout_ref[...])
    x2_ref[...] = x2
    ms = jnp.mean(x2 * x2, axis=-1, keepdims=True)
    xn_ref[...] = x2 * lax.rsqrt(ms + EPS) * gffn_ref[...]


def _merge(x, attn, rec, ga, gb, w_oa, w_ob, w_out, g_ffn, tm):
    n = x.shape[0]
    row = lambda i: (i, 0)
    fix = lambda i: (0, 0)
    return pl.pallas_call(
        _merge_kernel,
        grid=(n // tm,),
        in_specs=[pl.BlockSpec((tm, D_MODEL), row),
                  pl.BlockSpec((tm, D_ATTN), row),
                  pl.BlockSpec((tm, D_LRU), row),
                  pl.BlockSpec((tm, D_MODEL), row),
                  pl.BlockSpec((tm, D_MODEL), row),
                  pl.BlockSpec((D_ATTN, D_MODEL), fix),
                  pl.BlockSpec((D_LRU, D_MODEL), fix),
                  pl.BlockSpec((D_MODEL, D_MODEL), fix),
                  pl.BlockSpec((1, D_MODEL), fix)],
        out_specs=[pl.BlockSpec((tm, D_MODEL), row), pl.BlockSpec((tm, D_MODEL), row)],
        out_shape=[jax.ShapeDtypeStruct((n, D_MODEL), F32), jax.ShapeDtypeStruct((n, D_MODEL), F32)],
        compiler_params=_cparams(("parallel",)),
        name="merge",
    )(x, attn, rec, ga, gb, w_oa, w_ob, w_out, g_ffn)


_STAIR = [(i, PEER_TOPK // (i + 1)) for i in range(PEER_TOPK)]


def _route_kernel(xn_ref, wpq_ref, keys_ref, eid_ref, g_ref, *, tt):
    k = PEER_TOPK
    qt = _dot_nt(wpq_ref[...], xn_ref[...].astype(BF16))
    rio = lax.broadcasted_iota(I32, (PEER_NKEYS, tt), 0)

    def top_rows(cur, tie, payload, count, big):
        vals, pays = [], []
        for _ in range(count):
            m = jnp.max(cur, axis=0, keepdims=True)
            f = jnp.min(jnp.where(cur == m, tie, big), axis=0, keepdims=True)
            hit = tie == f
            vals.append(m)
            if payload is None:
                pays.append(f)
            else:
                pays.append(jnp.max(jnp.where(hit, payload, -1), axis=0, keepdims=True))
            cur = jnp.where(hit, NEG_INF, cur)
        return jnp.concatenate(vals, axis=0), jnp.concatenate(pays, axis=0)

    eids, gs = [], []
    for h in range(PEER_HEADS):
        sv, si = [], []
        for p in range(2):
            hp = h * 2 + p
            s = jnp.dot(keys_ref[hp], qt[hp * PEER_HALF:(hp + 1) * PEER_HALF, :].astype(BF16),
                        preferred_element_type=F32)
            v, ix = top_rows(s, rio, None, k, PEER_NKEYS)
            sv.append(v)
            si.append(ix)
        cands, cids, flats = [], [], []
        j8 = lax.broadcasted_iota(I32, (8, tt), 0)
        j16 = lax.broadcasted_iota(I32, (k, tt), 0)
        cands.append(sv[0][0:1] + sv[1])
        cids.append(si[0][0:1] * PEER_NKEYS + si[1])
        flats.append(j16)
        for i, nj in _STAIR[1:8]:
            cands.append(jnp.where(j8 < nj, sv[0][i:i + 1] + sv[1][0:8], NEG_INF))
            cids.append(si[0][i:i + 1] * PEER_NKEYS + si[1][0:8])
            flats.append(i * k + j8)
        cands.append(sv[0][8:16] + sv[1][0:1])
        cids.append(si[0][8:16] * PEER_NKEYS + si[1][0:1])
        flats.append((j8 + 8) * k)
        cand = jnp.concatenate(cands, axis=0)
        cid = jnp.concatenate(cids, axis=0)
        flat = jnp.concatenate(flats, axis=0)
        fv, eid = top_rows(cand, flat, cid, k, k * k)
        e = jnp.exp(fv - fv[0:1])
        gs.append(e / jnp.sum(e, axis=0, keepdims=True))
        eids.append(eid)
    eid_ref[...] = jnp.concatenate(eids, axis=0).T
    g_ref[...] = jnp.concatenate(gs, axis=0).T


def _route(xn, w_pq_t, keys, tt):
    n = xn.shape[0]
    kern = functools.partial(_route_kernel, tt=tt)
    return pl.pallas_call(
        kern,
        grid=(n // tt,),
        in_specs=[pl.BlockSpec((tt, D_MODEL), lambda i: (i, 0)),
                  pl.BlockSpec(w_pq_t.shape, lambda i: (0, 0)),
                  pl.BlockSpec(keys.shape, lambda i: (0, 0, 0))],
        out_specs=[pl.BlockSpec((tt, PEER_SEL), lambda i: (i, 0)),
                   pl.BlockSpec((tt, PEER_SEL), lambda i: (i, 0))],
        out_shape=[jax.ShapeDtypeStruct((n, PEER_SEL), I32),
                   jax.ShapeDtypeStruct((n, PEER_SEL), F32)],
        compiler_params=_cparams(("parallel",)),
        name="peer_route",
    )(xn, w_pq_t, keys)


def _unpack(words):
    lo = pltpu.bitcast(lax.shift_left(words, 16), F32)
    hi = pltpu.bitcast(words & jnp.int32(-65536), F32)
    return lo, hi


def _peer_u_kernel(eid_ref, x_ref, g_ref, tab_ref, w_ref, part_ref, act_ref, *, tt):
    lane_t = lax.broadcasted_iota(I32, (PEER_SEL, tt), 1)
    act_ref[...] = jnp.zeros_like(act_ref)

    def token(c, carry):
        r0 = pl.multiple_of(c * 8, 8)
        x_lo = x_ref[pl.ds(r0, SLAB), :]
        x_hi = x_ref[pl.ds(r0 + SLAB, SLAB), :]
        for kk in range(PEER_SEL):
            e0 = pl.multiple_of(eid_ref[c, kk] * SLAB, SLAB)
            lo, hi = _unpack(tab_ref[pl.ds(e0, SLAB), :])
            part_ref[kk * SLAB:(kk + 1) * SLAB, :] = lo * x_lo + hi * x_hi
        t = part_ref[pl.ds(0, PEER_SEL, stride=SLAB), :]
        for s in range(1, SLAB):
            t = t + part_ref[pl.ds(s, PEER_SEL, stride=SLAB), :]
        col = jnp.sum(t, axis=-1, keepdims=True)
        act_ref[...] = act_ref[...] + jnp.where(lane_t == c, col, 0.0)
        return carry

    lax.fori_loop(0, tt, token, 0)
    w_ref[...] = g_ref[...] * _gelu(act_ref[...].T)


def _peer_u(eid, xn8, g, tab, tt):
    n = eid.shape[0]
    kern = functools.partial(_peer_u_kernel, tt=tt)
    return pl.pallas_call(
        kern,
        grid=(n // tt,),
        in_specs=[pl.BlockSpec((tt, PEER_SEL), lambda i: (i, 0), memory_space=pltpu.SMEM),
                  pl.BlockSpec((tt * 8, LANES), lambda i: (i, 0)),
                  pl.BlockSpec((tt, PEER_SEL), lambda i: (i, 0)),
                  pl.BlockSpec(memory_space=pltpu.VMEM)],
        out_specs=pl.BlockSpec((tt, PEER_SEL), lambda i: (i, 0)),
        out_shape=jax.ShapeDtypeStruct((n, PEER_SEL), F32),
        scratch_shapes=[pltpu.VMEM((PEER_SEL * SLAB, LANES), F32),
                        pltpu.VMEM((PEER_SEL, tt), F32)],
        compiler_params=_cparams(("arbitrary",)),
        name="peer_u",
    )(eid, xn8, g, tab)


def _peer_v_kernel(eid_ref, w_ref, x2_ref, tab_ref, y_ref, *, tt):
    nacc = 4

    def token(c, carry):
        r0 = pl.multiple_of(c * 8, 8)
        acc_lo = [jnp.zeros((SLAB, LANES), F32) for _ in range(nacc)]
        acc_hi = [jnp.zeros((SLAB, LANES), F32) for _ in range(nacc)]
        for kk in range(PEER_SEL):
            e0 = pl.multiple_of(eid_ref[c, kk] * SLAB, SLAB)
            wk = w_ref[c, kk]
            lo, hi = _unpack(tab_ref[pl.ds(e0, SLAB), :])
            acc_lo[kk % nacc] = acc_lo[kk % nacc] + wk * lo
            acc_hi[kk % nacc] = acc_hi[kk % nacc] + wk * hi
        lo = (acc_lo[0] + acc_lo[1]) + (acc_lo[2] + acc_lo[3])
        hi = (acc_hi[0] + acc_hi[1]) + (acc_hi[2] + acc_hi[3])
        y_ref[pl.ds(r0, SLAB), :] = x2_ref[pl.ds(r0, SLAB), :] + lo
        y_ref[pl.ds(r0 + SLAB, SLAB), :] = x2_ref[pl.ds(r0 + SLAB, SLAB), :] + hi
        return carry

    lax.fori_loop(0, tt, token, 0)


def _peer_v(eid, w, x28, tab, tt):
    n = eid.shape[0]
    kern = functools.partial(_peer_v_kernel, tt=tt)
    return pl.pallas_call(
        kern,
        grid=(n // tt,),
        in_specs=[pl.BlockSpec((tt, PEER_SEL), lambda i: (i, 0), memory_space=pltpu.SMEM),
                  pl.BlockSpec((tt, PEER_SEL), lambda i: (i, 0), memory_space=pltpu.SMEM),
                  pl.BlockSpec((tt * 8, LANES), lambda i: (i, 0)),
                  pl.BlockSpec(memory_space=pltpu.VMEM)],
        out_specs=pl.BlockSpec((tt * 8, LANES), lambda i: (i, 0)),
        out_shape=jax.ShapeDtypeStruct((n * 8, LANES), F32),
        compiler_params=_cparams(("arbitrary",)),
        name="peer_v",
    )(eid, w, x28, tab)


def _pack_table(tab):
    e = tab.shape[0]
    bits = lax.bitcast_convert_type(tab.astype(BF16), jnp.uint16).astype(jnp.uint32)
    words = bits[:, :HALF_ROW_WORDS] | (bits[:, HALF_ROW_WORDS:] << 16)
    return lax.bitcast_convert_type(words, I32).reshape(e * SLAB, LANES)


def _tile(n, pref):
    t = pref
    while n % t:
        t //= 2
    return t


def _layer(x, conv0, h0, attend, wts, bsz, t):
    n = bsz * t
    xf = x.reshape(n, D_MODEL)
    q, k, v, xr, yr, ga, gb = _in_proj(xf, wts["g_mix"], wts["w_in"], wts["b_in"], wts["g_q"],
                                       wts["g_k"], wts["ones_bd"], _tile(n, 256))
    attn = attend(q, k, v)
    rec, h_last, conv_out = _rglru(xr, yr, conv0, h0.reshape(bsz, 1, D_LRU), wts["w_conv"],
                                   wts["b_conv"], wts["wa_bd"], wts["b_a"], wts["wx_bd"], wts["b_x"],
                                   wts["lam"], bsz, t, _tile(t, 512))
    x2, xn = _merge(xf, attn, rec, ga, gb, wts["w_oa"], wts["w_ob"], wts["w_out"], wts["g_ffn"],
                    _tile(n, 256))
    tt = _tile(n, 256)
    eid, g = _route(xn, wts["w_pq_t"], wts["keys"], tt)
    wgt = _peer_u(eid, xn.reshape(n * 8, LANES), g, wts["tab_u"], tt)
    y = _peer_v(eid, wgt, x2.reshape(n * 8, LANES), wts["tab_v"], tt)
    return (y.reshape(bsz, t, D_MODEL), k.reshape(bsz, t, N_HEADS, HEAD_DIM),
            v.reshape(bsz, t, N_HEADS, HEAD_DIM), h_last.reshape(bsz, D_LRU), conv_out)


def _block_diag(w):
    nb, bi, bo = w.shape
    eye = jnp.eye(nb, dtype=w.dtype)
    return (eye[:, None, :, None] * w[:, :, None, :]).reshape(nb * bi, nb * bo)


def kernel(x_prompt, x_sample, cache_k, cache_v, state_h, state_conv, page_table, g_mix, w_in, b_in,
           g_q, g_k, w_conv, b_conv, w_a, b_a, w_x, b_x, lam, w_oa, w_ob, w_out, g_ffn, w_pq,
           sub_keys, peer_u, peer_v):
    row = lambda a: a.reshape(1, -1)
    head_ones = jnp.ones((1, HEAD_DIM, HEAD_DIM), F32)
    wts = {
        "g_mix": row(g_mix), "w_in": w_in.astype(BF16), "b_in": row(b_in),
        "g_q": row(jnp.tile(g_q, N_HEADS)), "g_k": row(jnp.tile(g_k, N_HEADS)),
        "ones_bd": _block_diag(jnp.tile(head_ones, (N_HEADS, 1, 1))).astype(BF16),
        "w_conv": w_conv, "b_conv": row(b_conv),
        "wa_bd": _block_diag(w_a).astype(BF16), "b_a": row(b_a),
        "wx_bd": _block_diag(w_x).astype(BF16), "b_x": row(b_x), "lam": row(lam),
        "w_oa": w_oa.astype(BF16), "w_ob": w_ob.astype(BF16), "w_out": w_out.astype(BF16),
        "g_ffn": row(g_ffn), "w_pq_t": w_pq.T.astype(BF16),
        "keys": sub_keys.reshape(PEER_HEADS * 2, PEER_NKEYS, PEER_HALF).astype(BF16),
        "tab_u": _pack_table(peer_u), "tab_v": _pack_table(peer_v),
    }
    slopes = jnp.asarray(np.array([2.0 ** (-8.0 * (h + 1) / N_HEADS) for h in range(N_HEADS)],
                                  dtype=np.float32))
    bp, tp, _ = x_prompt.shape
    bs, ts, _ = x_sample.shape

    def attend_prompt(q, k, v):
        return _moba_prompt(q, k, v, slopes, bp, tp)

    def attend_sample(q, k, v):
        return _moba_sample(q, k, v, cache_k, cache_v, page_table, slopes, bs, ts)

    conv00 = jnp.zeros((bp, CONV_W - 1, D_LRU), x_prompt.dtype)
    h00 = jnp.zeros((bp, D_LRU), state_h.dtype)
    y_p, k_p, v_p, h_p, c_p = _layer(x_prompt, conv00, h00, attend_prompt, wts, bp, tp)
    y_s, k_s, v_s, h_s, c_s = _layer(x_sample, state_conv, state_h, attend_sample, wts, bs, ts)
    return (y_p, y_s, k_p, v_p, h_p, c_p, k_s, v_s, h_s, c_s)
```

```python
import functools
import math

import jax
import jax.numpy as jnp
import numpy as np
from jax import lax
from jax.experimental import pallas as pl
from jax.experimental.pallas import tpu as pltpu

F32 = jnp.float32
BF16 = jnp.bfloat16
I32 = jnp.int32

D_MODEL = 1024
N_HEADS = 8
HEAD_DIM = 64
D_ATTN = N_HEADS * HEAD_DIM
MOBA_BLOCK = 256
MOBA_TOPK = 3
PAGE_SIZE = 128
D_LRU = 512
LRU_BLOCKS = 8
CONV_W = 4
LRU_C = 8.0
PEER_HEADS = 8
PEER_NKEYS = 128
PEER_HALF = 128
PEER_TOPK = 16
PEER_SEL = PEER_HEADS * PEER_TOPK
EPS = 1e-6
NEG_INF = float("-inf")

LANES = 128
SUBLANES = 8
CHUNKS = D_MODEL // LANES
SLAB = CHUNKS // 2
VMEM_LIMIT = 56 * 1024 * 1024


def _cparams(sem):
    return pltpu.CompilerParams(dimension_semantics=sem, vmem_limit_bytes=VMEM_LIMIT)


def _gelu(x):
    c = math.sqrt(2.0 / math.pi)
    return x * (0.5 * (1.0 + jnp.tanh(c * (x + 0.044715 * (x * x * x)))))


def _dot(a, b):
    return jnp.dot(a.astype(BF16), b.astype(BF16), preferred_element_type=F32)


def _dot_nt(a, b, precision=None):
    return lax.dot_general(a, b, (((1,), (1,)), ((), ())), precision=precision,
                           preferred_element_type=F32)


def _split_bf16(x):
    hi = x.astype(BF16)
    lo = (x - hi.astype(F32)).astype(BF16)
    return hi, lo


def _inproj_kernel(x_ref, gmix_ref, w_ref, b_ref, gq_ref, gk_ref, ones_ref,
                   q_ref, k_ref, v_ref, xr_ref, yr_ref, ga_ref, gb_ref):
    x = x_ref[...]
    ms = jnp.mean(x * x, axis=-1, keepdims=True)
    xn = x * lax.rsqrt(ms + EPS) * gmix_ref[...]
    proj = _dot(xn, w_ref[...]) + b_ref[...]

    def head_norm(t, g):
        hi, lo = _split_bf16(t * t)
        s = (jnp.dot(hi, ones_ref[...], preferred_element_type=F32)
             + jnp.dot(lo, ones_ref[...], preferred_element_type=F32))
        return t * lax.rsqrt(s * (1.0 / HEAD_DIM) + EPS) * g

    a = D_ATTN
    q_ref[...] = head_norm(proj[:, 0:a], gq_ref[...])
    k_ref[...] = head_norm(proj[:, a:2 * a], gk_ref[...])
    v_ref[...] = proj[:, 2 * a:3 * a]
    xr_ref[...] = proj[:, 3 * a:3 * a + D_LRU]
    yr_ref[...] = proj[:, 3 * a + D_LRU:3 * a + 2 * D_LRU]
    o = 3 * a + 2 * D_LRU
    ga_ref[...] = proj[:, o:o + D_MODEL]
    gb_ref[...] = proj[:, o + D_MODEL:o + 2 * D_MODEL]


def _in_proj(x, gmix, w_in, b_in, gq, gk, ones_bd, tm):
    n = x.shape[0]
    d_in = w_in.shape[1]
    row = lambda i: (i, 0)
    fix = lambda i: (0, 0)
    outs = [D_ATTN, D_ATTN, D_ATTN, D_LRU, D_LRU, D_MODEL, D_MODEL]
    return pl.pallas_call(
        _inproj_kernel,
        grid=(n // tm,),
        in_specs=[pl.BlockSpec((tm, D_MODEL), row),
                  pl.BlockSpec((1, D_MODEL), fix),
                  pl.BlockSpec((D_MODEL, d_in), fix),
                  pl.BlockSpec((1, d_in), fix),
                  pl.BlockSpec((1, D_ATTN), fix),
                  pl.BlockSpec((1, D_ATTN), fix),
                  pl.BlockSpec((D_ATTN, D_ATTN), fix)],
        out_specs=[pl.BlockSpec((tm, w), row) for w in outs],
        out_shape=[jax.ShapeDtypeStruct((n, w), F32) for w in outs],
        compiler_params=_cparams(("parallel",)),
        name="in_proj",
    )(x, gmix, w_in, b_in, gq, gk, ones_bd)


def _moba_prompt_kernel(slopes_ref, q_ref, k_ref, v_ref, o_ref,
                        kb_ref, vt_ref, kmean_ref, bias2_ref, biaso_ref, selb_ref, qs_ref, *, nblk):
    hp = pl.program_id(1)
    i = pl.program_id(2)
    blk = MOBA_BLOCK
    rc = (lax.broadcasted_iota(I32, (blk, blk), 1)
          - lax.broadcasted_iota(I32, (blk, blk), 0)).astype(F32)

    @pl.when(i == 0)
    def _():
        for j in range(nblk):
            kj = k_ref[j * blk:(j + 1) * blk, :]
            kmean_ref[j:j + 1, :] = jnp.mean(kj, axis=0, keepdims=True)
            kb_ref[j * blk:(j + 1) * blk, :] = kj.astype(BF16)
            vt_ref[j // 2, :, (j % 2) * blk:(j % 2 + 1) * blk] = (
                v_ref[j * blk:(j + 1) * blk, :].T.astype(BF16))
        for hh in range(2):
            b = -slopes_ref[hp * 2 + hh] * rc
            cols = slice(hh * blk, (hh + 1) * blk)
            bias2_ref[0:blk, cols] = b
            bias2_ref[blk:2 * blk, cols] = b
            biaso_ref[:, cols] = jnp.where(rc >= 0.0, b, NEG_INF)

    q2 = q_ref[...]
    lane = lax.broadcasted_iota(I32, (blk, LANES), 1)
    jrow = lax.broadcasted_iota(I32, (nblk, blk), 0)
    for hh in range(2):
        qh = jnp.where((lane >= hh * HEAD_DIM) & (lane < (hh + 1) * HEAD_DIM), q2, 0.0)
        gate = _dot_nt(kmean_ref[...], qh, precision=lax.Precision.HIGHEST)
        cnt = jnp.zeros((nblk, blk), I32)
        for ii in range(nblk):
            gi = gate[ii:ii + 1, :]
            better = (gi > gate) | ((gi == gate) & (ii < jrow))
            cnt = cnt + jnp.where(better, jnp.where(ii < i, 1, 0), 0)
        selb_ref[:, hh * blk:(hh + 1) * blk] = jnp.where((cnt < MOBA_TOPK) & (jrow < i), 0.0, NEG_INF)
        qs_ref[hh * blk:(hh + 1) * blk, :] = (qh * (HEAD_DIM ** -0.5)).astype(BF16)

    col2 = lax.broadcasted_iota(I32, (1, 2 * blk), 1)
    slope_row = jnp.where(col2 < blk, slopes_ref[hp * 2], slopes_ref[hp * 2 + 1])

    def pv(vt, p, hh):
        return jnp.dot(vt[hh * HEAD_DIM:(hh + 1) * HEAD_DIM, :], p[:, hh * blk:(hh + 1) * blk],
                       preferred_element_type=F32)

    own0 = pl.multiple_of(i * blk, blk)
    s = _dot_nt(kb_ref[pl.ds(own0, blk), :], qs_ref[...]) + biaso_ref[...]
    m0 = jnp.max(s, axis=0, keepdims=True)
    p = jnp.exp(s - m0)
    l0 = jnp.sum(p, axis=0, keepdims=True)
    vto = v_ref[pl.ds(own0, blk), :].T.astype(BF16)
    pb = p.astype(BF16)

    def body(jp, carry):
        m_old, l_old, acc0, acc1 = carry
        k0 = pl.multiple_of(jp * 2 * blk, 2 * blk)
        s2 = _dot_nt(kb_ref[pl.ds(k0, 2 * blk), :], qs_ref[...]) + bias2_ref[...]
        sa, sb = s2[0:blk], s2[blk:2 * blk]
        ja = 2 * jp
        off_a = selb_ref[pl.ds(ja, 1), :] - slope_row * ((i - ja) * blk).astype(F32)
        off_b = selb_ref[pl.ds(ja + 1, 1), :] - slope_row * ((i - ja - 1) * blk).astype(F32)
        m_new = jnp.maximum(m_old, jnp.maximum(jnp.max(sa, axis=0, keepdims=True) + off_a,
                                               jnp.max(sb, axis=0, keepdims=True) + off_b))
        alpha = jnp.exp(m_old - m_new)
        pa = jnp.exp(sa - (m_new - off_a))
        pb2 = jnp.exp(sb - (m_new - off_b))
        l_new = (alpha * l_old + jnp.sum(pa, axis=0, keepdims=True)
                 + jnp.sum(pb2, axis=0, keepdims=True))
        pp = jnp.concatenate([pa, pb2], axis=0).astype(BF16)
        vt = vt_ref[jp]
        acc0 = alpha[:, 0:blk] * acc0 + pv(vt, pp, 0)
        acc1 = alpha[:, blk:2 * blk] * acc1 + pv(vt, pp, 1)
        return m_new, l_new, acc0, acc1

    _, l_fin, acc0, acc1 = lax.fori_loop(0, (i + 1) // 2, body,
                                         (m0, l0, pv(vto, pb, 0), pv(vto, pb, 1)))
    o_ref[...] = jnp.concatenate([acc0 / l_fin[:, 0:blk], acc1 / l_fin[:, blk:2 * blk]], axis=0).T


def _moba_prompt(q, k, v, slopes, bsz, t):
    nblk = t // MOBA_BLOCK
    assert nblk % 2 == 0
    blk = MOBA_BLOCK
    kern = functools.partial(_moba_prompt_kernel, nblk=nblk)
    return pl.pallas_call(
        kern,
        grid_spec=pltpu.PrefetchScalarGridSpec(
            num_scalar_prefetch=1,
            grid=(bsz, N_HEADS // 2, nblk),
            in_specs=[pl.BlockSpec((blk, LANES), lambda b, hp, i, s: (b * nblk + i, hp)),
                      pl.BlockSpec((t, LANES), lambda b, hp, i, s: (b, hp)),
                      pl.BlockSpec((t, LANES), lambda b, hp, i, s: (b, hp))],
            out_specs=pl.BlockSpec((blk, LANES), lambda b, hp, i, s: (b * nblk + i, hp)),
            scratch_shapes=[pltpu.VMEM((t, LANES), BF16),
                            pltpu.VMEM((nblk // 2, LANES, 2 * blk), BF16),
                            pltpu.VMEM((nblk, LANES), F32),
                            pltpu.VMEM((2 * blk, 2 * blk), F32),
                            pltpu.VMEM((blk, 2 * blk), F32),
                            pltpu.VMEM((nblk, 2 * blk), F32),
                            pltpu.VMEM((2 * blk, LANES), BF16)]),
        out_shape=jax.ShapeDtypeStruct((bsz * t, D_ATTN), F32),
        compiler_params=_cparams(("parallel", "parallel", "arbitrary")),
        name="moba_prompt",
    )(slopes, q, k, v)


def _moba_sample_kernel(pt_ref, slopes_ref, q_ref, kn_ref, vn_ref, k0_ref, k1_ref, v0_ref, v1_ref,
                        o_ref, qf_ref, base_ref, m_ref, l_ref, g_ref, oall_ref, *, nblk, tq, past):
    j = pl.program_id(1)
    rows = N_HEADS * tq
    blk = MOBA_BLOCK
    cols = blk * N_HEADS
    lane_s = lax.broadcasted_iota(I32, (rows, LANES), 1)

    def row_slopes(width):
        hrow = lax.broadcasted_iota(I32, (rows, width), 0) // tq
        slope = jnp.zeros((rows, width), F32)
        for h in range(N_HEADS):
            slope = jnp.where(hrow == h, slopes_ref[h], slope)
        return slope

    def by_head(x):
        return jnp.concatenate([x[:, h * HEAD_DIM:(h + 1) * HEAD_DIM] for h in range(N_HEADS)], axis=0)

    @pl.when(j == 0)
    def _():
        qf_ref[...] = by_head(q_ref[...])
        r = lax.broadcasted_iota(I32, (rows, cols), 0)
        c = lax.broadcasted_iota(I32, (rows, cols), 1)
        qpos = past + (r - (r // tq) * tq)
        dist = (qpos - c // N_HEADS).astype(F32)
        base_ref[...] = jnp.where(c % N_HEADS == r // tq, -row_slopes(cols) * dist, NEG_INF)
        m_ref[...] = jnp.zeros_like(m_ref)
        l_ref[...] = jnp.zeros_like(l_ref)
        g_ref[...] = jnp.zeros_like(g_ref)

    qf = qf_ref[...]
    qs = (qf * (HEAD_DIM ** -0.5)).astype(BF16)

    @pl.when(j < nblk)
    def _():
        per = PAGE_SIZE * N_HEADS
        k0 = k0_ref[0].reshape(per, HEAD_DIM)
        k1 = k1_ref[0].reshape(per, HEAD_DIM)
        kb = jnp.concatenate([k0, k1], axis=0).astype(BF16)
        vb = jnp.concatenate([v0_ref[0].reshape(per, HEAD_DIM),
                              v1_ref[0].reshape(per, HEAD_DIM)], axis=0).astype(BF16)
        s = _dot_nt(qs, kb) + base_ref[...]
        mj = jnp.max(s, axis=-1, keepdims=True)
        p = jnp.exp(s - mj)
        lj = jnp.sum(p, axis=-1, keepdims=True)
        oall_ref[j] = jnp.dot(p.astype(BF16), vb, preferred_element_type=F32)
        ksum = jnp.sum(k0_ref[0], axis=0) + jnp.sum(k1_ref[0], axis=0)
        kmean = ksum * (1.0 / blk)
        kmx = jnp.concatenate([jnp.broadcast_to(kmean[h:h + 1, :], (tq, HEAD_DIM))
                               for h in range(N_HEADS)], axis=0)
        gj = jnp.sum(qf * kmx, axis=-1, keepdims=True)
        here = lane_s == j
        mj = mj + row_slopes(1) * (j * blk).astype(F32)
        m_ref[...] = jnp.where(here, mj, m_ref[...])
        l_ref[...] = jnp.where(here, lj, l_ref[...])
        g_ref[...] = jnp.where(here, gj, g_ref[...])

    @pl.when(j == nblk)
    def _():
        knf = by_head(kn_ref[...])
        vnf = by_head(vn_ref[...])
        r = lax.broadcasted_iota(I32, (rows, rows), 0)
        c = lax.broadcasted_iota(I32, (rows, rows), 1)
        dist = ((r - (r // tq) * tq) - (c - (c // tq) * tq)).astype(F32)
        ok = (r // tq == c // tq) & (dist >= 0.0)
        s = _dot_nt(qs, knf.astype(BF16)) - row_slopes(rows) * dist
        s = jnp.where(ok, s, NEG_INF)
        m_o = jnp.max(s, axis=-1, keepdims=True)
        p = jnp.exp(s - m_o)
        l_o = jnp.sum(p, axis=-1, keepdims=True)
        o_o = _dot(p, vnf)
        gate = g_ref[...]
        cnt = jnp.zeros((rows, LANES), I32)
        for ii in range(nblk):
            gi = gate[:, ii:ii + 1]
            better = (gi > gate) | ((gi == gate) & (ii < lane_s))
            cnt = cnt + jnp.where(better, 1, 0)
        sel = (cnt < MOBA_TOPK) & (lane_s < nblk)
        m_all = m_ref[...]
        m_tot = jnp.maximum(m_o, jnp.max(jnp.where(sel, m_all, NEG_INF), axis=-1, keepdims=True))
        w = jnp.where(sel, jnp.exp(m_all - m_tot), 0.0)
        w_o = jnp.exp(m_o - m_tot)
        den = jnp.sum(w * l_ref[...], axis=-1, keepdims=True) + w_o * l_o
        num = w_o * o_o
        for jj in range(nblk):
            num = num + w[:, jj:jj + 1] * oall_ref[jj]
        res = num / den
        o_ref[...] = jnp.concatenate([res[h * tq:(h + 1) * tq, :] for h in range(N_HEADS)], axis=1)


def _moba_sample(q, kn, vn, cache_k, cache_v, page_table, slopes, bsz, tq):
    n_pages = page_table.shape[1]
    per = MOBA_BLOCK // PAGE_SIZE
    nblk = n_pages // per
    past = n_pages * PAGE_SIZE
    rows = N_HEADS * tq
    new = pl.BlockSpec((tq, D_ATTN), lambda b, j, pt, s: (b, 0))

    def page(which):
        return pl.BlockSpec(
            (1, PAGE_SIZE, N_HEADS, HEAD_DIM),
            lambda b, j, pt, s: (pt[b, per * jnp.minimum(j, nblk - 1) + which], 0, 0, 0))

    kern = functools.partial(_moba_sample_kernel, nblk=nblk, tq=tq, past=past)
    return pl.pallas_call(
        kern,
        grid_spec=pltpu.PrefetchScalarGridSpec(
            num_scalar_prefetch=2,
            grid=(bsz, nblk + 1),
            in_specs=[new, new, new, page(0), page(1), page(0), page(1)],
            out_specs=pl.BlockSpec((tq, D_ATTN), lambda b, j, pt, s: (b, 0)),
            scratch_shapes=[pltpu.VMEM((rows, HEAD_DIM), F32),
                            pltpu.VMEM((rows, MOBA_BLOCK * N_HEADS), F32),
                            pltpu.VMEM((rows, LANES), F32),
                            pltpu.VMEM((rows, LANES), F32),
                            pltpu.VMEM((rows, LANES), F32),
                            pltpu.VMEM((nblk, rows, HEAD_DIM), F32)]),
        out_shape=jax.ShapeDtypeStruct((bsz * tq, D_ATTN), F32),
        compiler_params=_cparams(("parallel", "arbitrary")),
        name="moba_sample",
    )(page_table, slopes, q, kn, vn, cache_k, cache_k, cache_v, cache_v)


def _rglru_kernel(xr_ref, yr_ref, conv0_ref, h0_ref, wconv_ref, bconv_ref, wa_ref, ba_ref,
                  wx_ref, bx_ref, lam_ref, rec_ref, hlast_ref, convout_ref,
                  xpad_ref, a_ref, b_ref, hs_ref, h_ref, *, tc):
    c = pl.program_id(1)
    nc = pl.num_programs(1)
    pad = 8
    keep = CONV_W - 1

    @pl.when(c == 0)
    def _():
        xpad_ref[pad - keep:pad, :] = conv0_ref[0]
        h_ref[...] = h0_ref[0]

    xpad_ref[pad:pad + tc, :] = xr_ref[...]
    xc = bconv_ref[...] + wconv_ref[0:1, :] * xpad_ref[pad - 3:pad - 3 + tc, :]
    for jj in range(1, CONV_W):
        xc = xc + wconv_ref[jj:jj + 1, :] * xpad_ref[pad - 3 + jj:pad - 3 + jj + tc, :]
    tail = xpad_ref[pad + tc - keep:pad + tc, :]
    xpad_ref[pad - keep:pad, :] = tail
    convout_ref[0] = tail

    r = jax.nn.sigmoid(_dot(xc, wa_ref[...]) + ba_ref[...])
    gi = jax.nn.sigmoid(_dot(xc, wx_ref[...]) + bx_ref[...])
    lam = lam_ref[...]
    softplus_neg = jnp.maximum(-lam, 0.0) + jnp.log1p(jnp.exp(-jnp.abs(lam)))
    log_a = (-LRU_C) * r * softplus_neg
    a_ref[...] = jnp.exp(log_a)
    th = jnp.tanh(log_a)
    b_ref[...] = jnp.sqrt(-2.0 * th / (1.0 - th)) * (gi * xc)

    def step(t8, h):
        t0 = pl.multiple_of(t8 * 8, 8)
        a8 = a_ref[pl.ds(t0, 8), :]
        b8 = b_ref[pl.ds(t0, 8), :]
        out = []
        for rr in range(8):
            h = a8[rr:rr + 1, :] * h + b8[rr:rr + 1, :]
            out.append(h)
        hs_ref[pl.ds(t0, 8), :] = jnp.concatenate(out, axis=0)
        return h

    h = lax.fori_loop(0, tc // 8, step, h_ref[...])
    h_ref[...] = h
    rec_ref[...] = hs_ref[...] * _gelu(yr_ref[...])

    @pl.when(c == nc - 1)
    def _():
        hlast_ref[0] = h


def _rglru(xr, yr, conv0, h0, w_conv, b_conv, wa_bd, b_a, wx_bd, b_x, lam, bsz, t, tc):
    ncs = t // tc
    row = lambda b, c: (b * ncs + c, 0)
    per_b = lambda b, c: (b, 0, 0)
    fix = lambda b, c: (0, 0)
    kern = functools.partial(_rglru_kernel, tc=tc)
    vec = pl.BlockSpec((1, D_LRU), fix)
    return pl.pallas_call(
        kern,
        grid=(bsz, ncs),
        in_specs=[pl.BlockSpec((tc, D_LRU), row),
                  pl.BlockSpec((tc, D_LRU), row),
                  pl.BlockSpec((1, CONV_W - 1, D_LRU), per_b),
                  pl.BlockSpec((1, 1, D_LRU), per_b),
                  pl.BlockSpec((CONV_W, D_LRU), fix),
                  vec,
                  pl.BlockSpec((D_LRU, D_LRU), fix), vec,
                  pl.BlockSpec((D_LRU, D_LRU), fix), vec,
                  vec],
        out_specs=[pl.BlockSpec((tc, D_LRU), row),
                   pl.BlockSpec((1, 1, D_LRU), per_b),
                   pl.BlockSpec((1, CONV_W - 1, D_LRU), per_b)],
        out_shape=[jax.ShapeDtypeStruct((bsz * t, D_LRU), F32),
                   jax.ShapeDtypeStruct((bsz, 1, D_LRU), F32),
                   jax.ShapeDtypeStruct((bsz, CONV_W - 1, D_LRU), F32)],
        scratch_shapes=[pltpu.VMEM((tc + 8, D_LRU), F32),
                        pltpu.VMEM((tc, D_LRU), F32),
                        pltpu.VMEM((tc, D_LRU), F32),
                        pltpu.VMEM((tc, D_LRU), F32),
                        pltpu.VMEM((1, D_LRU), F32)],
        compiler_params=_cparams(("parallel", "arbitrary")),
        name="rglru",
    )(xr, yr, conv0, h0, w_conv, b_conv, wa_bd, b_a, wx_bd, b_x, lam)


def _merge_kernel(x_ref, attn_ref, rec_ref, ga_ref, gb_ref, woa_ref, wob_ref, wout_ref, gffn_ref,
                  x2_ref, xn_ref):
    merged = (jax.nn.sigmoid(ga_ref[...]) * _dot(attn_ref[...], woa_ref[...])
              + jax.nn.sigmoid(gb_ref[...]) * _dot(rec_ref[...], wob_ref[...]))
    x2 = x_ref[...] + _dot(merged, wout_ref[...])
    x2_ref[...] = x2
    ms = jnp.mean(x2 * x2, axis=-1, keepdims=True)
    xn_ref[...] = x2 * lax.rsqrt(ms + EPS) * gffn_ref[...]


def _merge(x, attn, rec, ga, gb, w_oa, w_ob, w_out, g_ffn, tm):
    n = x.shape[0]
    row = lambda i: (i, 0)
    fix = lambda i: (0, 0)
    return pl.pallas_call(
        _merge_kernel,
        grid=(n // tm,),
        in_specs=[pl.BlockSpec((tm, D_MODEL), row),
                  pl.BlockSpec((tm, D_ATTN), row),
                  pl.BlockSpec((tm, D_LRU), row),
                  pl.BlockSpec((tm, D_MODEL), row),
                  pl.BlockSpec((tm, D_MODEL), row),
                  pl.BlockSpec((D_ATTN, D_MODEL), fix),
                  pl.BlockSpec((D_LRU, D_MODEL), fix),
                  pl.BlockSpec((D_MODEL, D_MODEL), fix),
                  pl.BlockSpec((1, D_MODEL), fix)],
        out_specs=[pl.BlockSpec((tm, D_MODEL), row), pl.BlockSpec((tm, D_MODEL), row)],
        out_shape=[jax.ShapeDtypeStruct((n, D_MODEL), F32), jax.ShapeDtypeStruct((n, D_MODEL), F32)],
        compiler_params=_cparams(("parallel",)),
        name="merge",
    )(x, attn, rec, ga, gb, w_oa, w_ob, w_out, g_ffn)


_STAIR = [(i, PEER_TOPK // (i + 1)) for i in range(PEER_TOPK)]


def _route_kernel(xn_ref, wpq_ref, keys_ref, eid_ref, g_ref, *, tt):
    k = PEER_TOPK
    qt = _dot_nt(wpq_ref[...], xn_ref[...].astype(BF16))
    rio = lax.broadcasted_iota(I32, (PEER_NKEYS, tt), 0).astype(F32)

    def top_rows(cur, tie, payload, count, big):
        vals, pays = [], []
        for _ in range(count):
            m = jnp.max(cur, axis=0, keepdims=True)
            f = jnp.min(jnp.where(cur == m, tie, big), axis=0, keepdims=True)
            hit = tie == f
            vals.append(m)
            if payload is None:
                pays.append(f)
            else:
                pays.append(jnp.max(jnp.where(hit, payload, -1.0), axis=0, keepdims=True))
            cur = jnp.where(hit, NEG_INF, cur)
        return jnp.concatenate(vals, axis=0), jnp.concatenate(pays, axis=0)

    eids, gs = [], []
    for h in range(PEER_HEADS):
        sv, si = [], []
        for p in range(2):
            hp = h * 2 + p
            s = jnp.dot(keys_ref[hp], qt[hp * PEER_HALF:(hp + 1) * PEER_HALF, :].astype(BF16),
                        preferred_element_type=F32)
            v, ix = top_rows(s, rio, None, k, float(PEER_NKEYS))
            sv.append(v)
            si.append(ix)
        cands, cids, flats = [], [], []
        j8 = lax.broadcasted_iota(I32, (8, tt), 0)
        j8f = j8.astype(F32)
        cands.append(sv[0][0:1] + sv[1])
        cids.append(si[0][0:1] * PEER_NKEYS + si[1])
        flats.append(lax.broadcasted_iota(I32, (k, tt), 0).astype(F32))
        for i, nj in _STAIR[1:8]:
            cands.append(jnp.where(j8 < nj, sv[0][i:i + 1] + sv[1][0:8], NEG_INF))
            cids.append(si[0][i:i + 1] * PEER_NKEYS + si[1][0:8])
            flats.append(float(i * k) + j8f)
        cands.append(sv[0][8:16] + sv[1][0:1])
        cids.append(si[0][8:16] * PEER_NKEYS + si[1][0:1])
        flats.append((j8f + 8.0) * k)
        cand = jnp.concatenate(cands, axis=0)
        cid = jnp.concatenate(cids, axis=0)
        flat = jnp.concatenate(flats, axis=0)
        fv, eid = top_rows(cand, flat, cid, k, float(k * k))
        e = jnp.exp(fv - fv[0:1])
        gs.append(e / jnp.sum(e, axis=0, keepdims=True))
        eids.append(eid * SLAB)
    eid_ref[...] = jnp.concatenate(eids, axis=0).T.astype(I32)
    g_ref[...] = jnp.concatenate(gs, axis=0).T


def _route(xn, w_pq_t, keys, tt):
    n = xn.shape[0]
    kern = functools.partial(_route_kernel, tt=tt)
    return pl.pallas_call(
        kern,
        grid=(n // tt,),
        in_specs=[pl.BlockSpec((tt, D_MODEL), lambda i: (i, 0)),
                  pl.BlockSpec(w_pq_t.shape, lambda i: (0, 0)),
                  pl.BlockSpec(keys.shape, lambda i: (0, 0, 0))],
        out_specs=[pl.BlockSpec((tt, PEER_SEL), lambda i: (i, 0)),
                   pl.BlockSpec((tt, PEER_SEL), lambda i: (i, 0))],
        out_shape=[jax.ShapeDtypeStruct((n, PEER_SEL), I32),
                   jax.ShapeDtypeStruct((n, PEER_SEL), F32)],
        compiler_params=_cparams(("parallel",)),
        name="peer_route",
    )(xn, w_pq_t, keys)


GROWS = PEER_SEL * CHUNKS


def _gather_rows(off_ref, tab_ref, gs_ref, c):
    row_ref = off_ref.at[c]
    for kk in range(PEER_SEL):
        e0 = pl.multiple_of(row_ref[kk], SLAB)
        gs_ref[kk * SLAB:(kk + 1) * SLAB, :] = tab_ref[pl.ds(e0, SLAB), :]


TOKENS_PER_TRIP = 8


def _pipelined_tokens(tt, gather, compute, gs0_ref, gs1_ref):
    bufs = (gs0_ref, gs1_ref)
    gather(0, gs0_ref)

    def trip(t, carry):
        c0 = t * TOKENS_PER_TRIP
        for u in range(TOKENS_PER_TRIP):
            gather(jnp.minimum(c0 + u + 1, tt - 1), bufs[(u + 1) % 2])
            compute(c0 + u, bufs[u % 2])
        return carry

    lax.fori_loop(0, tt // TOKENS_PER_TRIP, trip, 0)


def _peer_u_kernel(eid_ref, x8_ref, g_ref, tab_ref, fold_ref, w_ref, gs0_ref, gs1_ref, d_ref, *, tt):
    sub = lax.broadcasted_iota(I32, (2 * CHUNKS, GROWS), 0)
    col = lax.broadcasted_iota(I32, (2 * CHUNKS, GROWS), 1)
    diag = (col & (CHUNKS - 1)) == (sub & (CHUNKS - 1))

    def gather(c, gs_ref):
        _gather_rows(eid_ref, tab_ref, gs_ref, c)

    def compute(c, gs_ref):
        hi, lo = _split_bf16(x8_ref[pl.ds(pl.multiple_of(c * CHUNKS, CHUNKS), CHUNKS), :])
        xl = jnp.concatenate([hi, lo], axis=0)
        r = _dot_nt(xl, pltpu.bitcast(gs_ref[...], BF16))
        d_ref[pl.ds(c, 1), :] = jnp.sum(jnp.where(diag, r, 0.0), axis=0, keepdims=True)

    _pipelined_tokens(tt, gather, compute, gs0_ref, gs1_ref)
    hi, lo = _split_bf16(d_ref[...])
    act = (jnp.dot(hi, fold_ref[...], preferred_element_type=F32)
           + jnp.dot(lo, fold_ref[...], preferred_element_type=F32))
    w_ref[...] = g_ref[...] * _gelu(act)


def _peer_u(eid, xn8, g, tab, fold, tt):
    n = eid.shape[0]
    kern = functools.partial(_peer_u_kernel, tt=tt)
    return pl.pallas_call(
        kern,
        grid=(n // tt,),
        in_specs=[pl.BlockSpec((tt, PEER_SEL), lambda i: (i, 0), memory_space=pltpu.SMEM),
                  pl.BlockSpec((tt * CHUNKS, LANES), lambda i: (i, 0)),
                  pl.BlockSpec((tt, PEER_SEL), lambda i: (i, 0)),
                  pl.BlockSpec(memory_space=pltpu.VMEM),
                  pl.BlockSpec((GROWS, PEER_SEL), lambda i: (0, 0))],
        out_specs=pl.BlockSpec((tt, PEER_SEL), lambda i: (i, 0)),
        out_shape=jax.ShapeDtypeStruct((n, PEER_SEL), F32),
        scratch_shapes=[pltpu.VMEM((PEER_SEL * SLAB, LANES), I32),
                        pltpu.VMEM((PEER_SEL * SLAB, LANES), I32),
                        pltpu.VMEM((tt, GROWS), F32)],
        compiler_params=_cparams(("arbitrary",)),
        name="peer_u",
    )(eid, xn8, g, tab, fold)


def _peer_v_kernel(eid_ref, w_ref, x28_ref, tab_ref, spread_ref, y8_ref, gs0_ref, gs1_ref, w8_ref, *, tt):
    hi, lo = _split_bf16(w_ref[...])
    w8_ref[...] = (jnp.dot(hi, spread_ref[...], preferred_element_type=F32)
                   + jnp.dot(lo, spread_ref[...], preferred_element_type=F32))
    sub = lax.broadcasted_iota(I32, (CHUNKS, GROWS), 0)
    col = lax.broadcasted_iota(I32, (CHUNKS, GROWS), 1)
    diag = (col & (CHUNKS - 1)) == sub

    def gather(c, gs_ref):
        _gather_rows(eid_ref, tab_ref, gs_ref, c)

    def compute(c, gs_ref):
        hi, lo = _split_bf16(jnp.where(diag, w8_ref[pl.ds(c, 1), :], 0.0))
        lhs = jnp.concatenate([hi, lo], axis=0)
        o = jnp.dot(lhs, pltpu.bitcast(gs_ref[...], BF16), preferred_element_type=F32)
        r0 = pl.multiple_of(c * CHUNKS, CHUNKS)
        y8_ref[pl.ds(r0, CHUNKS), :] = x28_ref[pl.ds(r0, CHUNKS), :] + (o[0:CHUNKS] + o[CHUNKS:])

    _pipelined_tokens(tt, gather, compute, gs0_ref, gs1_ref)


def _peer_v(eid, w, x28, tab, spread, tt):
    n = eid.shape[0]
    kern = functools.partial(_peer_v_kernel, tt=tt)
    return pl.pallas_call(
        kern,
        grid=(n // tt,),
        in_specs=[pl.BlockSpec((tt, PEER_SEL), lambda i: (i, 0), memory_space=pltpu.SMEM),
                  pl.BlockSpec((tt, PEER_SEL), lambda i: (i, 0)),
                  pl.BlockSpec((tt * CHUNKS, LANES), lambda i: (i, 0)),
                  pl.BlockSpec(memory_space=pltpu.VMEM),
                  pl.BlockSpec((PEER_SEL, GROWS), lambda i: (0, 0))],
        out_specs=pl.BlockSpec((tt * CHUNKS, LANES), lambda i: (i, 0)),
        out_shape=jax.ShapeDtypeStruct((n * CHUNKS, LANES), F32),
        scratch_shapes=[pltpu.VMEM((PEER_SEL * SLAB, LANES), I32),
                        pltpu.VMEM((PEER_SEL * SLAB, LANES), I32),
                        pltpu.VMEM((tt, GROWS), F32)],
        compiler_params=_cparams(("arbitrary",)),
        name="peer_v",
    )(eid, w, x28, tab, spread)


def _pack_table(tab):
    e = tab.shape[0]
    bits = lax.bitcast_convert_type(tab.astype(BF16), jnp.uint16).astype(jnp.uint32)
    bits = bits.reshape(e, SLAB, 2, LANES)
    words = bits[:, :, 0, :] | (bits[:, :, 1, :] << 16)
    return lax.bitcast_convert_type(words, I32).reshape(e * SLAB, LANES)


def _tile(n, pref):
    t = pref
    while n % t:
        t //= 2
    return t


def _layer(x, conv0, h0, attend, wts, bsz, t):
    n = bsz * t
    xf = x.reshape(n, D_MODEL)
    q, k, v, xr, yr, ga, gb = _in_proj(xf, wts["g_mix"], wts["w_in"], wts["b_in"], wts["g_q"],
                                       wts["g_k"], wts["ones_bd"], _tile(n, 256))
    attn = attend(q, k, v)
    rec, h_last, conv_out = _rglru(xr, yr, conv0, h0.reshape(bsz, 1, D_LRU), wts["w_conv"],
                                   wts["b_conv"], wts["wa_bd"], wts["b_a"], wts["wx_bd"], wts["b_x"],
                                   wts["lam"], bsz, t, _tile(t, 512))
    x2, xn = _merge(xf, attn, rec, ga, gb, wts["w_oa"], wts["w_ob"], wts["w_out"], wts["g_ffn"],
                    _tile(n, 256))
    tt = _tile(n, 256)
    eid, g = _route(xn, wts["w_pq_t"], wts["keys"], tt)
    wgt = _peer_u(eid, xn.reshape(n * CHUNKS, LANES), g, wts["tab_u"], wts["fold"], tt)
    y = _peer_v(eid, wgt, x2.reshape(n * CHUNKS, LANES), wts["tab_v"], wts["fold"].T, tt)
    return (y.reshape(bsz, t, D_MODEL), k.reshape(bsz, t, N_HEADS, HEAD_DIM),
            v.reshape(bsz, t, N_HEADS, HEAD_DIM), h_last.reshape(bsz, D_LRU), conv_out)


def _block_diag(w):
    nb, bi, bo = w.shape
    eye = jnp.eye(nb, dtype=w.dtype)
    return (eye[:, None, :, None] * w[:, :, None, :]).reshape(nb * bi, nb * bo)


def kernel(x_prompt, x_sample, cache_k, cache_v, state_h, state_conv, page_table, g_mix, w_in, b_in,
           g_q, g_k, w_conv, b_conv, w_a, b_a, w_x, b_x, lam, w_oa, w_ob, w_out, g_ffn, w_pq,
           sub_keys, peer_u, peer_v):
    row = lambda a: a.reshape(1, -1)
    head_ones = jnp.ones((1, HEAD_DIM, HEAD_DIM), F32)
    wts = {
        "g_mix": row(g_mix), "w_in": w_in.astype(BF16), "b_in": row(b_in),
        "g_q": row(jnp.tile(g_q, N_HEADS)), "g_k": row(jnp.tile(g_k, N_HEADS)),
        "ones_bd": _block_diag(jnp.tile(head_ones, (N_HEADS, 1, 1))).astype(BF16),
        "w_conv": w_conv, "b_conv": row(b_conv),
        "wa_bd": _block_diag(w_a).astype(BF16), "b_a": row(b_a),
        "wx_bd": _block_diag(w_x).astype(BF16), "b_x": row(b_x), "lam": row(lam),
        "w_oa": w_oa.astype(BF16), "w_ob": w_ob.astype(BF16), "w_out": w_out.astype(BF16),
        "g_ffn": row(g_ffn), "w_pq_t": w_pq.T.astype(BF16),
        "keys": sub_keys.reshape(PEER_HEADS * 2, PEER_NKEYS, PEER_HALF).astype(BF16),
        "tab_u": _pack_table(peer_u), "tab_v": _pack_table(peer_v),
        "fold": jnp.repeat(jnp.eye(PEER_SEL, dtype=BF16), CHUNKS, axis=0),
    }
    slopes = jnp.asarray(np.array([2.0 ** (-8.0 * (h + 1) / N_HEADS) for h in range(N_HEADS)],
                                  dtype=np.float32))
    bp, tp, _ = x_prompt.shape
    bs, ts, _ = x_sample.shape

    def attend_prompt(q, k, v):
        return _moba_prompt(q, k, v, slopes, bp, tp)

    def attend_sample(q, k, v):
        return _moba_sample(q, k, v, cache_k, cache_v, page_table, slopes, bs, ts)

    conv00 = jnp.zeros((bp, CONV_W - 1, D_LRU), x_prompt.dtype)
    h00 = jnp.zeros((bp, D_LRU), state_h.dtype)
    y_p, k_p, v_p, h_p, c_p = _layer(x_prompt, conv00, h00, attend_prompt, wts, bp, tp)
    y_s, k_s, v_s, h_s, c_s = _layer(x_sample, state_conv, state_h, attend_sample, wts, bs, ts)
    return (y_p, y_s, k_p, v_p, h_p, c_p, k_s, v_s, h_s, c_s)
```

```python
import functools
import math

import jax
import jax.numpy as jnp
import numpy as np
from jax import lax
from jax.experimental import pallas as pl
from jax.experimental.pallas import tpu as pltpu

F32 = jnp.float32
BF16 = jnp.bfloat16
I32 = jnp.int32

D_MODEL = 1024
N_HEADS = 8
HEAD_DIM = 64
D_ATTN = N_HEADS * HEAD_DIM
MOBA_BLOCK = 256
MOBA_TOPK = 3
PAGE_SIZE = 128
D_LRU = 512
LRU_BLOCKS = 8
CONV_W = 4
LRU_C = 8.0
PEER_HEADS = 8
PEER_NKEYS = 128
PEER_HALF = 128
PEER_TOPK = 16
PEER_SEL = PEER_HEADS * PEER_TOPK
EPS = 1e-6
NEG_INF = float("-inf")

LANES = 128
SUBLANES = 8
CHUNKS = D_MODEL // LANES
SLAB = CHUNKS // 2
VMEM_LIMIT = 56 * 1024 * 1024


def _cparams(sem):
    return pltpu.CompilerParams(dimension_semantics=sem, vmem_limit_bytes=VMEM_LIMIT)


def _gelu(x):
    c = math.sqrt(2.0 / math.pi)
    return x * (0.5 * (1.0 + jnp.tanh(c * (x + 0.044715 * (x * x * x)))))


def _dot(a, b):
    return jnp.dot(a.astype(BF16), b.astype(BF16), preferred_element_type=F32)


def _dot_nt(a, b, precision=None):
    return lax.dot_general(a, b, (((1,), (1,)), ((), ())), precision=precision,
                           preferred_element_type=F32)


def _split_bf16(x):
    hi = x.astype(BF16)
    lo = (x - hi.astype(F32)).astype(BF16)
    return hi, lo


def _inproj_kernel(x_ref, gmix_ref, w_ref, b_ref, gq_ref, gk_ref, ones_ref,
                   q_ref, k_ref, v_ref, xr_ref, yr_ref, ga_ref, gb_ref):
    x = x_ref[...]
    ms = jnp.mean(x * x, axis=-1, keepdims=True)
    xn = x * lax.rsqrt(ms + EPS) * gmix_ref[...]
    proj = _dot(xn, w_ref[...]) + b_ref[...]

    def head_norm(t, g):
        hi, lo = _split_bf16(t * t)
        s = (jnp.dot(hi, ones_ref[...], preferred_element_type=F32)
             + jnp.dot(lo, ones_ref[...], preferred_element_type=F32))
        return t * lax.rsqrt(s * (1.0 / HEAD_DIM) + EPS) * g

    a = D_ATTN
    q_ref[...] = head_norm(proj[:, 0:a], gq_ref[...])
    k_ref[...] = head_norm(proj[:, a:2 * a], gk_ref[...])
    v_ref[...] = proj[:, 2 * a:3 * a]
    xr_ref[...] = proj[:, 3 * a:3 * a + D_LRU]
    yr_ref[...] = proj[:, 3 * a + D_LRU:3 * a + 2 * D_LRU]
    o = 3 * a + 2 * D_LRU
    ga_ref[...] = proj[:, o:o + D_MODEL]
    gb_ref[...] = proj[:, o + D_MODEL:o + 2 * D_MODEL]


def _in_proj(x, gmix, w_in, b_in, gq, gk, ones_bd, tm):
    n = x.shape[0]
    d_in = w_in.shape[1]
    row = lambda i: (i, 0)
    fix = lambda i: (0, 0)
    outs = [D_ATTN, D_ATTN, D_ATTN, D_LRU, D_LRU, D_MODEL, D_MODEL]
    return pl.pallas_call(
        _inproj_kernel,
        grid=(n // tm,),
        in_specs=[pl.BlockSpec((tm, D_MODEL), row),
                  pl.BlockSpec((1, D_MODEL), fix),
                  pl.BlockSpec((D_MODEL, d_in), fix),
                  pl.BlockSpec((1, d_in), fix),
                  pl.BlockSpec((1, D_ATTN), fix),
                  pl.BlockSpec((1, D_ATTN), fix),
                  pl.BlockSpec((D_ATTN, D_ATTN), fix)],
        out_specs=[pl.BlockSpec((tm, w), row) for w in outs],
        out_shape=[jax.ShapeDtypeStruct((n, w), F32) for w in outs],
        compiler_params=_cparams(("parallel",)),
        name="in_proj",
    )(x, gmix, w_in, b_in, gq, gk, ones_bd)


def _moba_prompt_kernel(slopes_ref, q_ref, k_ref, v_ref, o_ref,
                        kb_ref, vt_ref, kmean_ref, bias2_ref, biaso_ref, selb_ref, qs_ref, *, nblk):
    hp = pl.program_id(1)
    i = pl.program_id(2)
    blk = MOBA_BLOCK
    rc = (lax.broadcasted_iota(I32, (blk, blk), 1)
          - lax.broadcasted_iota(I32, (blk, blk), 0)).astype(F32)

    @pl.when(i == 0)
    def _():
        for j in range(nblk):
            kj = k_ref[j * blk:(j + 1) * blk, :]
            kmean_ref[j:j + 1, :] = jnp.mean(kj, axis=0, keepdims=True)
            kb_ref[j * blk:(j + 1) * blk, :] = kj.astype(BF16)
            vt_ref[j // 2, :, (j % 2) * blk:(j % 2 + 1) * blk] = (
                v_ref[j * blk:(j + 1) * blk, :].T.astype(BF16))
        for hh in range(2):
            b = -slopes_ref[hp * 2 + hh] * rc
            cols = slice(hh * blk, (hh + 1) * blk)
            bias2_ref[0:blk, cols] = b
            bias2_ref[blk:2 * blk, cols] = b
            biaso_ref[:, cols] = jnp.where(rc >= 0.0, b, NEG_INF)

    q2 = q_ref[...]
    lane = lax.broadcasted_iota(I32, (blk, LANES), 1)
    jrow = lax.broadcasted_iota(I32, (nblk, blk), 0)
    for hh in range(2):
        qh = jnp.where((lane >= hh * HEAD_DIM) & (lane < (hh + 1) * HEAD_DIM), q2, 0.0)
        gate = _dot_nt(kmean_ref[...], qh, precision=lax.Precision.HIGHEST)
        cnt = jnp.zeros((nblk, blk), I32)
        for ii in range(nblk):
            gi = gate[ii:ii + 1, :]
            better = (gi > gate) | ((gi == gate) & (ii < jrow))
            cnt = cnt + jnp.where(better, jnp.where(ii < i, 1, 0), 0)
        selb_ref[:, hh * blk:(hh + 1) * blk] = jnp.where((cnt < MOBA_TOPK) & (jrow < i), 0.0, NEG_INF)
        qs_ref[hh * blk:(hh + 1) * blk, :] = (qh * (HEAD_DIM ** -0.5)).astype(BF16)

    col2 = lax.broadcasted_iota(I32, (1, 2 * blk), 1)
    slope_row = jnp.where(col2 < blk, slopes_ref[hp * 2], slopes_ref[hp * 2 + 1])

    def pv(vt, p, hh):
        return jnp.dot(vt[hh * HEAD_DIM:(hh + 1) * HEAD_DIM, :], p[:, hh * blk:(hh + 1) * blk],
                       preferred_element_type=F32)

    own0 = pl.multiple_of(i * blk, blk)
    s = _dot_nt(kb_ref[pl.ds(own0, blk), :], qs_ref[...]) + biaso_ref[...]
    m0 = jnp.max(s, axis=0, keepdims=True)
    p = jnp.exp(s - m0)
    l0 = jnp.sum(p, axis=0, keepdims=True)
    vto = v_ref[pl.ds(own0, blk), :].T.astype(BF16)
    pb = p.astype(BF16)

    def body(jp, carry):
        m_old, l_old, acc0, acc1 = carry
        k0 = pl.multiple_of(jp * 2 * blk, 2 * blk)
        s2 = _dot_nt(kb_ref[pl.ds(k0, 2 * blk), :], qs_ref[...]) + bias2_ref[...]
        sa, sb = s2[0:blk], s2[blk:2 * blk]
        ja = 2 * jp
        off_a = selb_ref[pl.ds(ja, 1), :] - slope_row * ((i - ja) * blk).astype(F32)
        off_b = selb_ref[pl.ds(ja + 1, 1), :] - slope_row * ((i - ja - 1) * blk).astype(F32)
        m_new = jnp.maximum(m_old, jnp.maximum(jnp.max(sa, axis=0, keepdims=True) + off_a,
                                               jnp.max(sb, axis=0, keepdims=True) + off_b))
        alpha = jnp.exp(m_old - m_new)
        pa = jnp.exp(sa - (m_new - off_a))
        pb2 = jnp.exp(sb - (m_new - off_b))
        l_new = (alpha * l_old + jnp.sum(pa, axis=0, keepdims=True)
                 + jnp.sum(pb2, axis=0, keepdims=True))
        pp = jnp.concatenate([pa, pb2], axis=0).astype(BF16)
        vt = vt_ref[jp]
        acc0 = alpha[:, 0:blk] * acc0 + pv(vt, pp, 0)
        acc1 = alpha[:, blk:2 * blk] * acc1 + pv(vt, pp, 1)
        return m_new, l_new, acc0, acc1

    _, l_fin, acc0, acc1 = lax.fori_loop(0, (i + 1) // 2, body,
                                         (m0, l0, pv(vto, pb, 0), pv(vto, pb, 1)))
    o_ref[...] = jnp.concatenate([acc0 / l_fin[:, 0:blk], acc1 / l_fin[:, blk:2 * blk]], axis=0).T


def _moba_prompt(q, k, v, slopes, bsz, t):
    nblk = t // MOBA_BLOCK
    assert nblk % 2 == 0
    blk = MOBA_BLOCK
    kern = functools.partial(_moba_prompt_kernel, nblk=nblk)
    return pl.pallas_call(
        kern,
        grid_spec=pltpu.PrefetchScalarGridSpec(
            num_scalar_prefetch=1,
            grid=(bsz, N_HEADS // 2, nblk),
            in_specs=[pl.BlockSpec((blk, LANES), lambda b, hp, i, s: (b * nblk + i, hp)),
                      pl.BlockSpec((t, LANES), lambda b, hp, i, s: (b, hp)),
                      pl.BlockSpec((t, LANES), lambda b, hp, i, s: (b, hp))],
            out_specs=pl.BlockSpec((blk, LANES), lambda b, hp, i, s: (b * nblk + i, hp)),
            scratch_shapes=[pltpu.VMEM((t, LANES), BF16),
                            pltpu.VMEM((nblk // 2, LANES, 2 * blk), BF16),
                            pltpu.VMEM((nblk, LANES), F32),
                            pltpu.VMEM((2 * blk, 2 * blk), F32),
                            pltpu.VMEM((blk, 2 * blk), F32),
                            pltpu.VMEM((nblk, 2 * blk), F32),
                            pltpu.VMEM((2 * blk, LANES), BF16)]),
        out_shape=jax.ShapeDtypeStruct((bsz * t, D_ATTN), F32),
        compiler_params=_cparams(("parallel", "parallel", "arbitrary")),
        name="moba_prompt",
    )(slopes, q, k, v)


def _moba_sample_kernel(pt_ref, slopes_ref, q_ref, kn_ref, vn_ref, k0_ref, k1_ref, v0_ref, v1_ref,
                        o_ref, qb_ref, qbt_ref, base_ref, m_ref, l_ref, g_ref, oall_ref,
                        *, nblk, tq, past):
    j = pl.program_id(1)
    rows = N_HEADS * tq
    blk = MOBA_BLOCK
    lane_d = lax.broadcasted_iota(I32, (rows, D_ATTN), 1)
    row_d = lax.broadcasted_iota(I32, (rows, D_ATTN), 0)
    own_head = lane_d // HEAD_DIM == row_d // tq
    lane_s = lax.broadcasted_iota(I32, (rows, LANES), 1)

    def row_consts(width):
        r = lax.broadcasted_iota(I32, (rows, width), 0)
        hrow = r // tq
        slope = jnp.zeros((rows, width), F32)
        for h in range(N_HEADS):
            slope = jnp.where(hrow == h, slopes_ref[h], slope)
        qpos = (past + (r - hrow * tq)).astype(F32)
        return slope, qpos

    @pl.when(j == 0)
    def _():
        qb = jnp.where(own_head, jnp.concatenate([q_ref[...]] * N_HEADS, axis=0), 0.0)
        qb_ref[...] = qb
        qbt_ref[:, 0:rows] = qb.T
        slope, qpos = row_consts(blk)
        rel = lax.broadcasted_iota(I32, (rows, blk), 1).astype(F32)
        base_ref[...] = -slope * (qpos - rel)
        m_ref[...] = jnp.zeros_like(m_ref)
        l_ref[...] = jnp.zeros_like(l_ref)
        g_ref[...] = jnp.zeros_like(g_ref)

    qb = qb_ref[...]
    qs = (qb * (HEAD_DIM ** -0.5)).astype(BF16)

    @pl.when(j < nblk)
    def _():
        kt = jnp.concatenate([k0_ref[0], k1_ref[0]], axis=1)
        vt = jnp.concatenate([v0_ref[0], v1_ref[0]], axis=1)
        s = jnp.dot(qs, kt.astype(BF16), preferred_element_type=F32) + base_ref[...]
        mj = jnp.max(s, axis=-1, keepdims=True)
        p = jnp.exp(s - mj)
        lj = jnp.sum(p, axis=-1, keepdims=True)
        oall_ref[j] = _dot_nt(p.astype(BF16), vt.astype(BF16))
        kcol = jnp.sum(kt, axis=1, keepdims=True) * (1.0 / blk)
        g_ref[pl.ds(j, 1), :] = jnp.sum(qbt_ref[...] * kcol, axis=0, keepdims=True)
        here = lane_s == j
        slope1, _ = row_consts(1)
        mj = mj + slope1 * (j * blk).astype(F32)
        m_ref[...] = jnp.where(here, mj, m_ref[...])
        l_ref[...] = jnp.where(here, lj, l_ref[...])

    @pl.when(j == nblk)
    def _():
        slope, qpos = row_consts(tq)
        kpos = (past + lax.broadcasted_iota(I32, (rows, tq), 1)).astype(F32)
        dist = qpos - kpos
        s = _dot_nt(qs, kn_ref[...].astype(BF16)) - slope * dist
        s = jnp.where(dist >= 0.0, s, NEG_INF)
        m_o = jnp.max(s, axis=-1, keepdims=True)
        p = jnp.exp(s - m_o)
        l_o = jnp.sum(p, axis=-1, keepdims=True)
        o_o = _dot(p, vn_ref[...])
        gate = g_ref[...]
        jrow = lax.broadcasted_iota(I32, (LANES, LANES), 0)
        cnt = jnp.zeros((LANES, LANES), I32)
        for ii in range(nblk):
            gi = gate[ii:ii + 1, :]
            better = (gi > gate) | ((gi == gate) & (ii < jrow))
            cnt = cnt + jnp.where(better, 1, 0)
        sel_t = jnp.where((cnt < MOBA_TOPK) & (jrow < nblk), 1.0, 0.0)
        sel = sel_t.T[0:rows, :] > 0.0
        m_all = m_ref[...]
        m_tot = jnp.maximum(m_o, jnp.max(jnp.where(sel, m_all, NEG_INF), axis=-1, keepdims=True))
        w = jnp.where(sel, jnp.exp(m_all - m_tot), 0.0)
        w_o = jnp.exp(m_o - m_tot)
        den = jnp.sum(w * l_ref[...], axis=-1, keepdims=True) + w_o * l_o
        num = w_o * o_o
        for jj in range(nblk):
            num = num + w[:, jj:jj + 1] * oall_ref[jj]
        res = jnp.where(own_head, num / den, 0.0)
        out = res[0:tq]
        for h in range(1, N_HEADS):
            out = out + res[h * tq:(h + 1) * tq]
        o_ref[...] = out


def _moba_sample(q, kn, vn, cache_k, cache_v, page_table, slopes, bsz, tq):
    n_pages = page_table.shape[1]
    per = MOBA_BLOCK // PAGE_SIZE
    nblk = n_pages // per
    past = n_pages * PAGE_SIZE
    rows = N_HEADS * tq
    assert nblk <= LANES and rows <= LANES
    n_pool = cache_k.shape[0]
    ckt = cache_k.transpose(0, 2, 3, 1).reshape(n_pool, D_ATTN, PAGE_SIZE)
    cvt = cache_v.transpose(0, 2, 3, 1).reshape(n_pool, D_ATTN, PAGE_SIZE)
    new = pl.BlockSpec((tq, D_ATTN), lambda b, j, pt, s: (b, 0))

    def page(which):
        return pl.BlockSpec(
            (1, D_ATTN, PAGE_SIZE),
            lambda b, j, pt, s: (pt[b, per * jnp.minimum(j, nblk - 1) + which], 0, 0))

    kern = functools.partial(_moba_sample_kernel, nblk=nblk, tq=tq, past=past)
    return pl.pallas_call(
        kern,
        grid_spec=pltpu.PrefetchScalarGridSpec(
            num_scalar_prefetch=2,
            grid=(bsz, nblk + 1),
            in_specs=[new, new, new, page(0), page(1), page(0), page(1)],
            out_specs=pl.BlockSpec((tq, D_ATTN), lambda b, j, pt, s: (b, 0)),
            scratch_shapes=[pltpu.VMEM((rows, D_ATTN), F32),
                            pltpu.VMEM((D_ATTN, LANES), F32),
                            pltpu.VMEM((rows, MOBA_BLOCK), F32),
                            pltpu.VMEM((rows, LANES), F32),
                            pltpu.VMEM((rows, LANES), F32),
                            pltpu.VMEM((LANES, LANES), F32),
                            pltpu.VMEM((nblk, rows, D_ATTN), F32)]),
        out_shape=jax.ShapeDtypeStruct((bsz * tq, D_ATTN), F32),
        compiler_params=_cparams(("parallel", "arbitrary")),
        name="moba_sample",
    )(page_table, slopes, q, kn, vn, ckt, ckt, cvt, cvt)


def _rglru_kernel(xr_ref, yr_ref, conv0_ref, h0_ref, wconv_ref, bconv_ref, wa_ref, ba_ref,
                  wx_ref, bx_ref, lam_ref, rec_ref, hlast_ref, convout_ref,
                  xpad_ref, a_ref, b_ref, hs_ref, h_ref, *, tc):
    c = pl.program_id(1)
    nc = pl.num_programs(1)
    pad = 8
    keep = CONV_W - 1

    @pl.when(c == 0)
    def _():
        xpad_ref[pad - keep:pad, :] = conv0_ref[0]
        h_ref[...] = h0_ref[0]

    xpad_ref[pad:pad + tc, :] = xr_ref[...]
    xc = bconv_ref[...] + wconv_ref[0:1, :] * xpad_ref[pad - 3:pad - 3 + tc, :]
    for jj in range(1, CONV_W):
        xc = xc + wconv_ref[jj:jj + 1, :] * xpad_ref[pad - 3 + jj:pad - 3 + jj + tc, :]
    tail = xpad_ref[pad + tc - keep:pad + tc, :]
    xpad_ref[pad - keep:pad, :] = tail
    convout_ref[0] = tail

    r = jax.nn.sigmoid(_dot(xc, wa_ref[...]) + ba_ref[...])
    gi = jax.nn.sigmoid(_dot(xc, wx_ref[...]) + bx_ref[...])
    lam = lam_ref[...]
    softplus_neg = jnp.maximum(-lam, 0.0) + jnp.log1p(jnp.exp(-jnp.abs(lam)))
    log_a = (-LRU_C) * r * softplus_neg
    a_ref[...] = jnp.exp(log_a)
    th = jnp.tanh(log_a)
    b_ref[...] = jnp.sqrt(-2.0 * th / (1.0 - th)) * (gi * xc)

    def step(t8, h):
        t0 = pl.multiple_of(t8 * 8, 8)
        a8 = a_ref[pl.ds(t0, 8), :]
        b8 = b_ref[pl.ds(t0, 8), :]
        out = []
        for rr in range(8):
            h = a8[rr:rr + 1, :] * h + b8[rr:rr + 1, :]
            out.append(h)
        hs_ref[pl.ds(t0, 8), :] = jnp.concatenate(out, axis=0)
        return h

    h = lax.fori_loop(0, tc // 8, step, h_ref[...])
    h_ref[...] = h
    rec_ref[...] = hs_ref[...] * _gelu(yr_ref[...])

    @pl.when(c == nc - 1)
    def _():
        hlast_ref[0] = h


def _rglru(xr, yr, conv0, h0, w_conv, b_conv, wa_bd, b_a, wx_bd, b_x, lam, bsz, t, tc):
    ncs = t // tc
    row = lambda b, c: (b * ncs + c, 0)
    per_b = lambda b, c: (b, 0, 0)
    fix = lambda b, c: (0, 0)
    kern = functools.partial(_rglru_kernel, tc=tc)
    vec = pl.BlockSpec((1, D_LRU), fix)
    return pl.pallas_call(
        kern,
        grid=(bsz, ncs),
        in_specs=[pl.BlockSpec((tc, D_LRU), row),
                  pl.BlockSpec((tc, D_LRU), row),
                  pl.BlockSpec((1, CONV_W - 1, D_LRU), per_b),
                  pl.BlockSpec((1, 1, D_LRU), per_b),
                  pl.BlockSpec((CONV_W, D_LRU), fix),
                  vec,
                  pl.BlockSpec((D_LRU, D_LRU), fix), vec,
                  pl.BlockSpec((D_LRU, D_LRU), fix), vec,
                  vec],
        out_specs=[pl.BlockSpec((tc, D_LRU), row),
                   pl.BlockSpec((1, 1, D_LRU), per_b),
                   pl.BlockSpec((1, CONV_W - 1, D_LRU), per_b)],
        out_shape=[jax.ShapeDtypeStruct((bsz * t, D_LRU), F32),
                   jax.ShapeDtypeStruct((bsz, 1, D_LRU), F32),
                   jax.ShapeDtypeStruct((bsz, CONV_W - 1, D_LRU), F32)],
        scratch_shapes=[pltpu.VMEM((tc + 8, D_LRU), F32),
                        pltpu.VMEM((tc, D_LRU), F32),
                        pltpu.VMEM((tc, D_LRU), F32),
                        pltpu.VMEM((tc, D_LRU), F32),
                        pltpu.VMEM((1, D_LRU), F32)],
        compiler_params=_cparams(("parallel", "arbitrary")),
        name="rglru",
    )(xr, yr, conv0, h0, w_conv, b_conv, wa_bd, b_a, wx_bd, b_x, lam)


def _merge_kernel(x_ref, attn_ref, rec_ref, ga_ref, gb_ref, woa_ref, wob_ref, wout_ref, gffn_ref,
                  x2_ref, xn_ref):
    merged = (jax.nn.sigmoid(ga_ref[...]) * _dot(attn_ref[...], woa_ref[...])
              + jax.nn.sigmoid(gb_ref[...]) * _dot(rec_ref[...], wob_ref[...]))
    x2 = x_ref[...] + _dot(merged, wout_ref[...])
    x2_ref[...] = x2
    ms = jnp.mean(x2 * x2, axis=-1, keepdims=True)
    xn_ref[...] = x2 * lax.rsqrt(ms + EPS) * gffn_ref[...]


def _merge(x, attn, rec, ga, gb, w_oa, w_ob, w_out, g_ffn, tm):
    n = x.shape[0]
    row = lambda i: (i, 0)
    fix = lambda i: (0, 0)
    return pl.pallas_call(
        _merge_kernel,
        grid=(n // tm,),
        in_specs=[pl.BlockSpec((tm, D_MODEL), row),
                  pl.BlockSpec((tm, D_ATTN), row),
                  pl.BlockSpec((tm, D_LRU), row),
                  pl.BlockSpec((tm, D_MODEL), row),
                  pl.BlockSpec((tm, D_MODEL), row),
                  pl.BlockSpec((D_ATTN, D_MODEL), fix),
                  pl.BlockSpec((D_LRU, D_MODEL), fix),
                  pl.BlockSpec((D_MODEL, D_MODEL), fix),
                  pl.BlockSpec((1, D_MODEL), fix)],
        out_specs=[pl.BlockSpec((tm, D_MODEL), row), pl.BlockSpec((tm, D_MODEL), row)],
        out_shape=[jax.ShapeDtypeStruct((n, D_MODEL), F32), jax.ShapeDtypeStruct((n, D_MODEL), F32)],
        compiler_params=_cparams(("parallel",)),
        name="merge",
    )(x, attn, rec, ga, gb, w_oa, w_ob, w_out, g_ffn)


_STAIR = [(i, PEER_TOPK // (i + 1)) for i in range(PEER_TOPK)]


def _route_kernel(xn_ref, wpq_ref, keys_ref, eid_ref, g_ref, *, tt):
    k = PEER_TOPK
    qt = _dot_nt(wpq_ref[...], xn_ref[...].astype(BF16))
    rio = lax.broadcasted_iota(I32, (PEER_NKEYS, tt), 0).astype(F32)

    def top_rows(cur, tie, payload, count, big):
        vals, pays = [], []
        for _ in range(count):
            m = jnp.max(cur, axis=0, keepdims=True)
            f = jnp.min(jnp.where(cur == m, tie, big), axis=0, keepdims=True)
            hit = tie == f
            vals.append(m)
            if payload is None:
                pays.append(f)
            else:
                pays.append(jnp.max(jnp.where(hit, payload, -1.0), axis=0, keepdims=True))
            cur = jnp.where(hit, NEG_INF, cur)
        return jnp.concatenate(vals, axis=0), jnp.concatenate(pays, axis=0)

    eids, gs = [], []
    for h in range(PEER_HEADS):
        sv, si = [], []
        for p in range(2):
            hp = h * 2 + p
            s = jnp.dot(keys_ref[hp], qt[hp * PEER_HALF:(hp + 1) * PEER_HALF, :].astype(BF16),
                        preferred_element_type=F32)
            v, ix = top_rows(s, rio, None, k, float(PEER_NKEYS))
            sv.append(v)
            si.append(ix)
        cands, cids, flats = [], [], []
        j8 = lax.broadcasted_iota(I32, (8, tt), 0)
        j8f = j8.astype(F32)
        cands.append(sv[0][0:1] + sv[1])
        cids.append(si[0][0:1] * PEER_NKEYS + si[1])
        flats.append(lax.broadcasted_iota(I32, (k, tt), 0).astype(F32))
        for i, nj in _STAIR[1:8]:
            cands.append(jnp.where(j8 < nj, sv[0][i:i + 1] + sv[1][0:8], NEG_INF))
            cids.append(si[0][i:i + 1] * PEER_NKEYS + si[1][0:8])
            flats.append(float(i * k) + j8f)
        cands.append(sv[0][8:16] + sv[1][0:1])
        cids.append(si[0][8:16] * PEER_NKEYS + si[1][0:1])
        flats.append((j8f + 8.0) * k)
        cand = jnp.concatenate(cands, axis=0)
        cid = jnp.concatenate(cids, axis=0)
        flat = jnp.concatenate(flats, axis=0)
        fv, eid = top_rows(cand, flat, cid, k, float(k * k))
        e = jnp.exp(fv - fv[0:1])
        gs.append(e / jnp.sum(e, axis=0, keepdims=True))
        eids.append(eid * SLAB)
    eid_ref[...] = jnp.concatenate(eids, axis=0).T.astype(I32)
    g_ref[...] = jnp.concatenate(gs, axis=0).T


def _route(xn, w_pq_t, keys, tt):
    n = xn.shape[0]
    kern = functools.partial(_route_kernel, tt=tt)
    return pl.pallas_call(
        kern,
        grid=(n // tt,),
        in_specs=[pl.BlockSpec((tt, D_MODEL), lambda i: (i, 0)),
                  pl.BlockSpec(w_pq_t.shape, lambda i: (0, 0)),
                  pl.BlockSpec(keys.shape, lambda i: (0, 0, 0))],
        out_specs=[pl.BlockSpec((tt, PEER_SEL), lambda i: (i, 0)),
                   pl.BlockSpec((tt, PEER_SEL), lambda i: (i, 0))],
        out_shape=[jax.ShapeDtypeStruct((n, PEER_SEL), I32),
                   jax.ShapeDtypeStruct((n, PEER_SEL), F32)],
        compiler_params=_cparams(("parallel",)),
        name="peer_route",
    )(xn, w_pq_t, keys)


GROWS = PEER_SEL * CHUNKS


def _gather_rows(off_ref, tab_ref, gs_ref, c):
    row_ref = off_ref.at[c]
    for kk in range(PEER_SEL):
        e0 = pl.multiple_of(row_ref[kk], SLAB)
        gs_ref[kk * SLAB:(kk + 1) * SLAB, :] = tab_ref[pl.ds(e0, SLAB), :]


TOKENS_PER_TRIP = 16


def _pipelined_tokens(tt, gather, compute, gs0_ref, gs1_ref):
    bufs = (gs0_ref, gs1_ref)
    gather(0, gs0_ref)

    def trip(t, carry):
        c0 = t * TOKENS_PER_TRIP
        for u in range(TOKENS_PER_TRIP):
            gather(jnp.minimum(c0 + u + 1, tt - 1), bufs[(u + 1) % 2])
            compute(c0 + u, bufs[u % 2])
        return carry

    lax.fori_loop(0, tt // TOKENS_PER_TRIP, trip, 0)


def _peer_u_kernel(eid_ref, x8_ref, g_ref, tab_ref, fold_ref, w_ref, gs0_ref, gs1_ref, d_ref, *, tt):
    sub = lax.broadcasted_iota(I32, (2 * CHUNKS, GROWS), 0)
    col = lax.broadcasted_iota(I32, (2 * CHUNKS, GROWS), 1)
    diag = (col & (CHUNKS - 1)) == (sub & (CHUNKS - 1))

    def gather(c, gs_ref):
        _gather_rows(eid_ref, tab_ref, gs_ref, c)

    def compute(c, gs_ref):
        hi, lo = _split_bf16(x8_ref[pl.ds(pl.multiple_of(c * CHUNKS, CHUNKS), CHUNKS), :])
        xl = jnp.concatenate([hi, lo], axis=0)
        r = _dot_nt(xl, pltpu.bitcast(gs_ref[...], BF16))
        d_ref[pl.ds(c, 1), :] = jnp.sum(jnp.where(diag, r, 0.0), axis=0, keepdims=True)

    _pipelined_tokens(tt, gather, compute, gs0_ref, gs1_ref)
    hi, lo = _split_bf16(d_ref[...])
    act = (jnp.dot(hi, fold_ref[...], preferred_element_type=F32)
           + jnp.dot(lo, fold_ref[...], preferred_element_type=F32))
    w_ref[...] = g_ref[...] * _gelu(act)


def _peer_u(eid, xn8, g, tab, fold, tt):
    n = eid.shape[0]
    kern = functools.partial(_peer_u_kernel, tt=tt)
    return pl.pallas_call(
        kern,
        grid=(n // tt,),
        in_specs=[pl.BlockSpec((tt, PEER_SEL), lambda i: (i, 0), memory_space=pltpu.SMEM),
                  pl.BlockSpec((tt * CHUNKS, LANES), lambda i: (i, 0)),
                  pl.BlockSpec((tt, PEER_SEL), lambda i: (i, 0)),
                  pl.BlockSpec(memory_space=pltpu.VMEM),
                  pl.BlockSpec((GROWS, PEER_SEL), lambda i: (0, 0))],
        out_specs=pl.BlockSpec((tt, PEER_SEL), lambda i: (i, 0)),
        out_shape=jax.ShapeDtypeStruct((n, PEER_SEL), F32),
        scratch_shapes=[pltpu.VMEM((PEER_SEL * SLAB, LANES), I32),
                        pltpu.VMEM((PEER_SEL * SLAB, LANES), I32),
                        pltpu.VMEM((tt, GROWS), F32)],
        compiler_params=_cparams(("arbitrary",)),
        name="peer_u",
    )(eid, xn8, g, tab, fold)


def _peer_v_kernel(eid_ref, w_ref, x28_ref, tab_ref, spread_ref, y8_ref, gs0_ref, gs1_ref, w8_ref, *, tt):
    hi, lo = _split_bf16(w_ref[...])
    w8_ref[...] = (jnp.dot(hi, spread_ref[...], preferred_element_type=F32)
                   + jnp.dot(lo, spread_ref[...], preferred_element_type=F32))
    sub = lax.broadcasted_iota(I32, (CHUNKS, GROWS), 0)
    col = lax.broadcasted_iota(I32, (CHUNKS, GROWS), 1)
    diag = (col & (CHUNKS - 1)) == sub

    def gather(c, gs_ref):
        _gather_rows(eid_ref, tab_ref, gs_ref, c)

    def compute(c, gs_ref):
        hi, lo = _split_bf16(jnp.where(diag, w8_ref[pl.ds(c, 1), :], 0.0))
        lhs = jnp.concatenate([hi, lo], axis=0)
        o = jnp.dot(lhs, pltpu.bitcast(gs_ref[...], BF16), preferred_element_type=F32)
        r0 = pl.multiple_of(c * CHUNKS, CHUNKS)
        y8_ref[pl.ds(r0, CHUNKS), :] = x28_ref[pl.ds(r0, CHUNKS), :] + (o[0:CHUNKS] + o[CHUNKS:])

    _pipelined_tokens(tt, gather, compute, gs0_ref, gs1_ref)


def _peer_v(eid, w, x28, tab, spread, tt):
    n = eid.shape[0]
    kern = functools.partial(_peer_v_kernel, tt=tt)
    return pl.pallas_call(
        kern,
        grid=(n // tt,),
        in_specs=[pl.BlockSpec((tt, PEER_SEL), lambda i: (i, 0), memory_space=pltpu.SMEM),
                  pl.BlockSpec((tt, PEER_SEL), lambda i: (i, 0)),
                  pl.BlockSpec((tt * CHUNKS, LANES), lambda i: (i, 0)),
                  pl.BlockSpec(memory_space=pltpu.VMEM),
                  pl.BlockSpec((PEER_SEL, GROWS), lambda i: (0, 0))],
        out_specs=pl.BlockSpec((tt * CHUNKS, LANES), lambda i: (i, 0)),
        out_shape=jax.ShapeDtypeStruct((n * CHUNKS, LANES), F32),
        scratch_shapes=[pltpu.VMEM((PEER_SEL * SLAB, LANES), I32),
                        pltpu.VMEM((PEER_SEL * SLAB, LANES), I32),
                        pltpu.VMEM((tt, GROWS), F32)],
        compiler_params=_cparams(("arbitrary",)),
        name="peer_v",
    )(eid, w, x28, tab, spread)


def _pack_table(tab):
    e = tab.shape[0]
    bits = lax.bitcast_convert_type(tab.astype(BF16), jnp.uint16).astype(jnp.uint32)
    bits = bits.reshape(e, SLAB, 2, LANES)
    words = bits[:, :, 0, :] | (bits[:, :, 1, :] << 16)
    return lax.bitcast_convert_type(words, I32).reshape(e * SLAB, LANES)


def _tile(n, pref):
    t = pref
    while n % t:
        t //= 2
    return t


def _layer(x, conv0, h0, attend, wts, bsz, t):
    n = bsz * t
    xf = x.reshape(n, D_MODEL)
    q, k, v, xr, yr, ga, gb = _in_proj(xf, wts["g_mix"], wts["w_in"], wts["b_in"], wts["g_q"],
                                       wts["g_k"], wts["ones_bd"], _tile(n, 256))
    attn = attend(q, k, v)
    rec, h_last, conv_out = _rglru(xr, yr, conv0, h0.reshape(bsz, 1, D_LRU), wts["w_conv"],
                                   wts["b_conv"], wts["wa_bd"], wts["b_a"], wts["wx_bd"], wts["b_x"],
                                   wts["lam"], bsz, t, _tile(t, 512))
    x2, xn = _merge(xf, attn, rec, ga, gb, wts["w_oa"], wts["w_ob"], wts["w_out"], wts["g_ffn"],
                    _tile(n, 256))
    tt = _tile(n, 256)
    eid, g = _route(xn, wts["w_pq_t"], wts["keys"], tt)
    wgt = _peer_u(eid, xn.reshape(n * CHUNKS, LANES), g, wts["tab_u"], wts["fold"], tt)
    y = _peer_v(eid, wgt, x2.reshape(n * CHUNKS, LANES), wts["tab_v"], wts["fold"].T, tt)
    return (y.reshape(bsz, t, D_MODEL), k.reshape(bsz, t, N_HEADS, HEAD_DIM),
            v.reshape(bsz, t, N_HEADS, HEAD_DIM), h_last.reshape(bsz, D_LRU), conv_out)


def _block_diag(w):
    nb, bi, bo = w.shape
    eye = jnp.eye(nb, dtype=w.dtype)
    return (eye[:, None, :, None] * w[:, :, None, :]).reshape(nb * bi, nb * bo)


def kernel(x_prompt, x_sample, cache_k, cache_v, state_h, state_conv, page_table, g_mix, w_in, b_in,
           g_q, g_k, w_conv, b_conv, w_a, b_a, w_x, b_x, lam, w_oa, w_ob, w_out, g_ffn, w_pq,
           sub_keys, peer_u, peer_v):
    row = lambda a: a.reshape(1, -1)
    head_ones = jnp.ones((1, HEAD_DIM, HEAD_DIM), F32)
    wts = {
        "g_mix": row(g_mix), "w_in": w_in.astype(BF16), "b_in": row(b_in),
        "g_q": row(jnp.tile(g_q, N_HEADS)), "g_k": row(jnp.tile(g_k, N_HEADS)),
        "ones_bd": _block_diag(jnp.tile(head_ones, (N_HEADS, 1, 1))).astype(BF16),
        "w_conv": w_conv, "b_conv": row(b_conv),
        "wa_bd": _block_diag(w_a).astype(BF16), "b_a": row(b_a),
        "wx_bd": _block_diag(w_x).astype(BF16), "b_x": row(b_x), "lam": row(lam),
        "w_oa": w_oa.astype(BF16), "w_ob": w_ob.astype(BF16), "w_out": w_out.astype(BF16),
        "g_ffn": row(g_ffn), "w_pq_t": w_pq.T.astype(BF16),
        "keys": sub_keys.reshape(PEER_HEADS * 2, PEER_NKEYS, PEER_HALF).astype(BF16),
        "tab_u": _pack_table(peer_u), "tab_v": _pack_table(peer_v),
        "fold": jnp.repeat(jnp.eye(PEER_SEL, dtype=BF16), CHUNKS, axis=0),
    }
    slopes = jnp.asarray(np.array([2.0 ** (-8.0 * (h + 1) / N_HEADS) for h in range(N_HEADS)],
                                  dtype=np.float32))
    bp, tp, _ = x_prompt.shape
    bs, ts, _ = x_sample.shape

    def attend_prompt(q, k, v):
        return _moba_prompt(q, k, v, slopes, bp, tp)

    def attend_sample(q, k, v):
        return _moba_sample(q, k, v, cache_k, cache_v, page_table, slopes, bs, ts)

    conv00 = jnp.zeros((bp, CONV_W - 1, D_LRU), x_prompt.dtype)
    h00 = jnp.zeros((bp, D_LRU), state_h.dtype)
    y_p, k_p, v_p, h_p, c_p = _layer(x_prompt, conv00, h00, attend_prompt, wts, bp, tp)
    y_s, k_s, v_s, h_s, c_s = _layer(x_sample, state_conv, state_h, attend_sample, wts, bs, ts)
    return (y_p, y_s, k_p, v_p, h_p, c_p, k_s, v_s, h_s, c_s)
```

```python
import functools
import math

import jax
import jax.numpy as jnp
import numpy as np
from jax import lax
from jax.experimental import pallas as pl
from jax.experimental.pallas import tpu as pltpu

F32 = jnp.float32
BF16 = jnp.bfloat16
I32 = jnp.int32

D_MODEL = 1024
N_HEADS = 8
HEAD_DIM = 64
D_ATTN = N_HEADS * HEAD_DIM
MOBA_BLOCK = 256
MOBA_TOPK = 3
PAGE_SIZE = 128
D_LRU = 512
LRU_BLOCKS = 8
CONV_W = 4
LRU_C = 8.0
PEER_HEADS = 8
PEER_NKEYS = 128
PEER_HALF = 128
PEER_TOPK = 16
PEER_SEL = PEER_HEADS * PEER_TOPK
EPS = 1e-6
NEG_INF = float("-inf")

LANES = 128
SUBLANES = 8
CHUNKS = D_MODEL // LANES
SLAB = CHUNKS // 2
VMEM_LIMIT = 56 * 1024 * 1024


def _cparams(sem):
    return pltpu.CompilerParams(dimension_semantics=sem, vmem_limit_bytes=VMEM_LIMIT)


def _gelu(x):
    c = math.sqrt(2.0 / math.pi)
    return x * (0.5 * (1.0 + jnp.tanh(c * (x + 0.044715 * (x * x * x)))))


def _dot(a, b):
    return jnp.dot(a.astype(BF16), b.astype(BF16), preferred_element_type=F32)


def _dot_nt(a, b, precision=None):
    return lax.dot_general(a, b, (((1,), (1,)), ((), ())), precision=precision,
                           preferred_element_type=F32)


def _split_bf16(x):
    hi = x.astype(BF16)
    lo = (x - hi.astype(F32)).astype(BF16)
    return hi, lo


def _inproj_kernel(x_ref, gmix_ref, w_ref, b_ref, gq_ref, gk_ref, ones_ref,
                   q_ref, k_ref, v_ref, xr_ref, yr_ref, ga_ref, gb_ref):
    x = x_ref[...]
    ms = jnp.mean(x * x, axis=-1, keepdims=True)
    xn = x * lax.rsqrt(ms + EPS) * gmix_ref[...]
    proj = _dot(xn, w_ref[...]) + b_ref[...]

    def head_norm(t, g):
        hi, lo = _split_bf16(t * t)
        s = (jnp.dot(hi, ones_ref[...], preferred_element_type=F32)
             + jnp.dot(lo, ones_ref[...], preferred_element_type=F32))
        return t * lax.rsqrt(s * (1.0 / HEAD_DIM) + EPS) * g

    a = D_ATTN
    q_ref[...] = head_norm(proj[:, 0:a], gq_ref[...])
    k_ref[...] = head_norm(proj[:, a:2 * a], gk_ref[...])
    v_ref[...] = proj[:, 2 * a:3 * a]
    xr_ref[...] = proj[:, 3 * a:3 * a + D_LRU]
    yr_ref[...] = proj[:, 3 * a + D_LRU:3 * a + 2 * D_LRU]
    o = 3 * a + 2 * D_LRU
    ga_ref[...] = proj[:, o:o + D_MODEL]
    gb_ref[...] = proj[:, o + D_MODEL:o + 2 * D_MODEL]


def _in_proj(x, gmix, w_in, b_in, gq, gk, ones_bd, tm):
    n = x.shape[0]
    d_in = w_in.shape[1]
    row = lambda i: (i, 0)
    fix = lambda i: (0, 0)
    outs = [D_ATTN, D_ATTN, D_ATTN, D_LRU, D_LRU, D_MODEL, D_MODEL]
    return pl.pallas_call(
        _inproj_kernel,
        grid=(n // tm,),
        in_specs=[pl.BlockSpec((tm, D_MODEL), row),
                  pl.BlockSpec((1, D_MODEL), fix),
                  pl.BlockSpec((D_MODEL, d_in), fix),
                  pl.BlockSpec((1, d_in), fix),
                  pl.BlockSpec((1, D_ATTN), fix),
                  pl.BlockSpec((1, D_ATTN), fix),
                  pl.BlockSpec((D_ATTN, D_ATTN), fix)],
        out_specs=[pl.BlockSpec((tm, w), row) for w in outs],
        out_shape=[jax.ShapeDtypeStruct((n, w), F32) for w in outs],
        compiler_params=_cparams(("parallel",)),
        name="in_proj",
    )(x, gmix, w_in, b_in, gq, gk, ones_bd)


def _moba_prompt_kernel(slopes_ref, q_ref, k_ref, v_ref, o_ref,
                        kb_ref, vt_ref, kmean_ref, bias2_ref, biaso_ref, selb_ref, qs_ref, sa_ref, sb_ref,
                        *, nblk):
    hp = pl.program_id(1)
    i = pl.program_id(2)
    blk = MOBA_BLOCK
    rc = (lax.broadcasted_iota(I32, (blk, blk), 1)
          - lax.broadcasted_iota(I32, (blk, blk), 0)).astype(F32)

    @pl.when(i == 0)
    def _():
        for j in range(nblk):
            kj = k_ref[j * blk:(j + 1) * blk, :]
            kmean_ref[j:j + 1, :] = jnp.mean(kj, axis=0, keepdims=True)
            kb_ref[j * blk:(j + 1) * blk, :] = kj.astype(BF16)
            vt_ref[j // 2, :, (j % 2) * blk:(j % 2 + 1) * blk] = (
                v_ref[j * blk:(j + 1) * blk, :].T.astype(BF16))
        for hh in range(2):
            b = -slopes_ref[hp * 2 + hh] * rc
            cols = slice(hh * blk, (hh + 1) * blk)
            bias2_ref[0:blk, cols] = b
            bias2_ref[blk:2 * blk, cols] = b
            biaso_ref[:, cols] = jnp.where(rc >= 0.0, b, NEG_INF)

    q2 = q_ref[...]
    lane = lax.broadcasted_iota(I32, (blk, LANES), 1)
    jrow = lax.broadcasted_iota(I32, (nblk, blk), 0)
    for hh in range(2):
        qh = jnp.where((lane >= hh * HEAD_DIM) & (lane < (hh + 1) * HEAD_DIM), q2, 0.0)
        gate = _dot_nt(kmean_ref[...], qh, precision=lax.Precision.HIGHEST)
        cnt = jnp.zeros((nblk, blk), I32)
        for ii in range(nblk):
            gi = gate[ii:ii + 1, :]
            better = (gi > gate) | ((gi == gate) & (ii < jrow))
            cnt = cnt + jnp.where(better, jnp.where(ii < i, 1, 0), 0)
        selb_ref[:, hh * blk:(hh + 1) * blk] = jnp.where((cnt < MOBA_TOPK) & (jrow < i), 0.0, NEG_INF)
        qs_ref[hh * blk:(hh + 1) * blk, :] = (qh * (HEAD_DIM ** -0.5)).astype(BF16)

    col2 = lax.broadcasted_iota(I32, (1, 2 * blk), 1)
    slope_row = jnp.where(col2 < blk, slopes_ref[hp * 2], slopes_ref[hp * 2 + 1])

    def pv(vt, p, hh):
        return jnp.dot(vt[hh * HEAD_DIM:(hh + 1) * HEAD_DIM, :], p[:, hh * blk:(hh + 1) * blk],
                       preferred_element_type=F32)

    own0 = pl.multiple_of(i * blk, blk)
    s = _dot_nt(kb_ref[pl.ds(own0, blk), :], qs_ref[...]) + biaso_ref[...]
    m0 = jnp.max(s, axis=0, keepdims=True)
    p = jnp.exp(s - m0)
    l0 = jnp.sum(p, axis=0, keepdims=True)
    vto = v_ref[pl.ds(own0, blk), :].T.astype(BF16)
    pb = p.astype(BF16)

    npairs = nblk // 2

    def scores(jp):
        k0 = pl.multiple_of(jp * 2 * blk, 2 * blk)
        return _dot_nt(kb_ref[pl.ds(k0, 2 * blk), :], qs_ref[...])

    def softmax_pv(jp, s_ref, carry):
        m_old, l_old, acc0, acc1 = carry
        s2 = s_ref[...] + bias2_ref[...]
        sa, sb = s2[0:blk], s2[blk:2 * blk]
        ja = 2 * jp
        off_a = selb_ref[pl.ds(ja, 1), :] - slope_row * ((i - ja) * blk).astype(F32)
        off_b = selb_ref[pl.ds(ja + 1, 1), :] - slope_row * ((i - ja - 1) * blk).astype(F32)
        m_new = jnp.maximum(m_old, jnp.maximum(jnp.max(sa, axis=0, keepdims=True) + off_a,
                                               jnp.max(sb, axis=0, keepdims=True) + off_b))
        alpha = jnp.exp(m_old - m_new)
        pa = jnp.exp(sa - (m_new - off_a))
        pb2 = jnp.exp(sb - (m_new - off_b))
        l_new = (alpha * l_old + jnp.sum(pa, axis=0, keepdims=True)
                 + jnp.sum(pb2, axis=0, keepdims=True))
        pp = jnp.concatenate([pa, pb2], axis=0).astype(BF16)
        vt = vt_ref[jp]
        acc0 = alpha[:, 0:blk] * acc0 + pv(vt, pp, 0)
        acc1 = alpha[:, blk:2 * blk] * acc1 + pv(vt, pp, 1)
        return m_new, l_new, acc0, acc1

    ntrips = (i + 1) // 2
    sa_ref[...] = scores(0)

    def body(t, carry):
        sb_ref[...] = scores(2 * t + 1)
        carry = softmax_pv(2 * t, sa_ref, carry)
        sa_ref[...] = scores(jnp.minimum(2 * t + 2, npairs - 1))
        return softmax_pv(2 * t + 1, sb_ref, carry)

    carry = lax.fori_loop(0, ntrips // 2, body, (m0, l0, pv(vto, pb, 0), pv(vto, pb, 1)))
    _, l_fin, acc0, acc1 = lax.cond(ntrips % 2 == 1,
                                    lambda c: softmax_pv(ntrips - 1, sa_ref, c),
                                    lambda c: c, carry)
    o_ref[...] = jnp.concatenate([acc0 / l_fin[:, 0:blk], acc1 / l_fin[:, blk:2 * blk]], axis=0).T


def _moba_prompt(q, k, v, slopes, bsz, t):
    nblk = t // MOBA_BLOCK
    assert nblk % 2 == 0
    blk = MOBA_BLOCK
    kern = functools.partial(_moba_prompt_kernel, nblk=nblk)
    return pl.pallas_call(
        kern,
        grid_spec=pltpu.PrefetchScalarGridSpec(
            num_scalar_prefetch=1,
            grid=(bsz, N_HEADS // 2, nblk),
            in_specs=[pl.BlockSpec((blk, LANES), lambda b, hp, i, s: (b * nblk + i, hp)),
                      pl.BlockSpec((t, LANES), lambda b, hp, i, s: (b, hp)),
                      pl.BlockSpec((t, LANES), lambda b, hp, i, s: (b, hp))],
            out_specs=pl.BlockSpec((blk, LANES), lambda b, hp, i, s: (b * nblk + i, hp)),
            scratch_shapes=[pltpu.VMEM((t, LANES), BF16),
                            pltpu.VMEM((nblk // 2, LANES, 2 * blk), BF16),
                            pltpu.VMEM((nblk, LANES), F32),
                            pltpu.VMEM((2 * blk, 2 * blk), F32),
                            pltpu.VMEM((blk, 2 * blk), F32),
                            pltpu.VMEM((nblk, 2 * blk), F32),
                            pltpu.VMEM((2 * blk, LANES), BF16),
                            pltpu.VMEM((2 * blk, 2 * blk), F32),
                            pltpu.VMEM((2 * blk, 2 * blk), F32)]),
        out_shape=jax.ShapeDtypeStruct((bsz * t, D_ATTN), F32),
        compiler_params=_cparams(("parallel", "parallel", "arbitrary")),
        name="moba_prompt",
    )(slopes, q, k, v)


SAMPLE_BLOCKS_PER_STEP = 2


def _moba_sample_kernel(pt_ref, slopes_ref, q_ref, kn_ref, vn_ref, *refs, nblk, tq, past):
    npage = SAMPLE_BLOCKS_PER_STEP * (MOBA_BLOCK // PAGE_SIZE)
    k_refs, v_refs = refs[0:npage], refs[npage:2 * npage]
    o_ref, qb_ref, qbt_ref, base_ref, m_ref, l_ref, g_ref, oall_ref = refs[2 * npage:]
    js = pl.program_id(1)
    nsteps = nblk // SAMPLE_BLOCKS_PER_STEP
    rows = N_HEADS * tq
    blk = MOBA_BLOCK
    lane_d = lax.broadcasted_iota(I32, (rows, D_ATTN), 1)
    row_d = lax.broadcasted_iota(I32, (rows, D_ATTN), 0)
    own_head = lane_d // HEAD_DIM == row_d // tq
    lane_s = lax.broadcasted_iota(I32, (rows, LANES), 1)

    def row_consts(width):
        r = lax.broadcasted_iota(I32, (rows, width), 0)
        hrow = r // tq
        slope = jnp.zeros((rows, width), F32)
        for h in range(N_HEADS):
            slope = jnp.where(hrow == h, slopes_ref[h], slope)
        qpos = (past + (r - hrow * tq)).astype(F32)
        return slope, qpos

    @pl.when(js == 0)
    def _():
        qb = jnp.where(own_head, jnp.concatenate([q_ref[...]] * N_HEADS, axis=0), 0.0)
        qb_ref[...] = qb
        qbt_ref[:, 0:rows] = qb.T
        slope, qpos = row_consts(blk)
        rel = lax.broadcasted_iota(I32, (rows, blk), 1).astype(F32)
        base_ref[...] = -slope * (qpos - rel)
        m_ref[...] = jnp.zeros_like(m_ref)
        l_ref[...] = jnp.zeros_like(l_ref)
        g_ref[...] = jnp.zeros_like(g_ref)

    qb = qb_ref[...]
    qs = (qb * (HEAD_DIM ** -0.5)).astype(BF16)

    def past_block(j, k_pages, v_pages):
        kt = jnp.concatenate([r[0] for r in k_pages], axis=1)
        vt = jnp.concatenate([r[0] for r in v_pages], axis=1)
        s = jnp.dot(qs, kt.astype(BF16), preferred_element_type=F32) + base_ref[...]
        mj = jnp.max(s, axis=-1, keepdims=True)
        p = jnp.exp(s - mj)
        lj = jnp.sum(p, axis=-1, keepdims=True)
        oall_ref[j] = _dot_nt(p.astype(BF16), vt.astype(BF16))
        kcol = jnp.sum(kt, axis=1, keepdims=True) * (1.0 / blk)
        g_ref[pl.ds(j, 1), :] = jnp.sum(qbt_ref[...] * kcol, axis=0, keepdims=True)
        here = lane_s == j
        slope1, _ = row_consts(1)
        mj = mj + slope1 * (j * blk).astype(F32)
        m_ref[...] = jnp.where(here, mj, m_ref[...])
        l_ref[...] = jnp.where(here, lj, l_ref[...])

    @pl.when(js < nsteps)
    def _():
        per = MOBA_BLOCK // PAGE_SIZE
        for a in range(SAMPLE_BLOCKS_PER_STEP):
            past_block(js * SAMPLE_BLOCKS_PER_STEP + a,
                       k_refs[a * per:(a + 1) * per], v_refs[a * per:(a + 1) * per])

    @pl.when(js == nsteps)
    def _():
        slope, qpos = row_consts(tq)
        kpos = (past + lax.broadcasted_iota(I32, (rows, tq), 1)).astype(F32)
        dist = qpos - kpos
        s = _dot_nt(qs, kn_ref[...].astype(BF16)) - slope * dist
        s = jnp.where(dist >= 0.0, s, NEG_INF)
        m_o = jnp.max(s, axis=-1, keepdims=True)
        p = jnp.exp(s - m_o)
        l_o = jnp.sum(p, axis=-1, keepdims=True)
        o_o = _dot(p, vn_ref[...])
        gate = g_ref[...]
        jrow = lax.broadcasted_iota(I32, (LANES, LANES), 0)
        cnt = jnp.zeros((LANES, LANES), I32)
        for ii in range(nblk):
            gi = gate[ii:ii + 1, :]
            better = (gi > gate) | ((gi == gate) & (ii < jrow))
            cnt = cnt + jnp.where(better, 1, 0)
        sel_t = jnp.where((cnt < MOBA_TOPK) & (jrow < nblk), 1.0, 0.0)
        sel = sel_t.T[0:rows, :] > 0.0
        m_all = m_ref[...]
        m_tot = jnp.maximum(m_o, jnp.max(jnp.where(sel, m_all, NEG_INF), axis=-1, keepdims=True))
        w = jnp.where(sel, jnp.exp(m_all - m_tot), 0.0)
        w_o = jnp.exp(m_o - m_tot)
        den = jnp.sum(w * l_ref[...], axis=-1, keepdims=True) + w_o * l_o
        num = w_o * o_o
        for jj in range(nblk):
            num = num + w[:, jj:jj + 1] * oall_ref[jj]
        res = jnp.where(own_head, num / den, 0.0)
        out = res[0:tq]
        for h in range(1, N_HEADS):
            out = out + res[h * tq:(h + 1) * tq]
        o_ref[...] = out


def _moba_sample(q, kn, vn, cache_k, cache_v, page_table, slopes, bsz, tq):
    n_pages = page_table.shape[1]
    per = MOBA_BLOCK // PAGE_SIZE
    nblk = n_pages // per
    past = n_pages * PAGE_SIZE
    rows = N_HEADS * tq
    assert nblk <= LANES and rows <= LANES and nblk % SAMPLE_BLOCKS_PER_STEP == 0
    nsteps = nblk // SAMPLE_BLOCKS_PER_STEP
    npage = SAMPLE_BLOCKS_PER_STEP * per
    n_pool = cache_k.shape[0]
    ckt = cache_k.transpose(0, 2, 3, 1).reshape(n_pool, D_ATTN, PAGE_SIZE)
    cvt = cache_v.transpose(0, 2, 3, 1).reshape(n_pool, D_ATTN, PAGE_SIZE)
    new = pl.BlockSpec((tq, D_ATTN), lambda b, j, pt, s: (b, 0))

    def page(which):
        return pl.BlockSpec(
            (1, D_ATTN, PAGE_SIZE),
            lambda b, j, pt, s: (pt[b, npage * jnp.minimum(j, nsteps - 1) + which], 0, 0))

    pages = [page(w) for w in range(npage)]
    kern = functools.partial(_moba_sample_kernel, nblk=nblk, tq=tq, past=past)
    return pl.pallas_call(
        kern,
        grid_spec=pltpu.PrefetchScalarGridSpec(
            num_scalar_prefetch=2,
            grid=(bsz, nsteps + 1),
            in_specs=[new, new, new] + pages + pages,
            out_specs=pl.BlockSpec((tq, D_ATTN), lambda b, j, pt, s: (b, 0)),
            scratch_shapes=[pltpu.VMEM((rows, D_ATTN), F32),
                            pltpu.VMEM((D_ATTN, LANES), F32),
                            pltpu.VMEM((rows, MOBA_BLOCK), F32),
                            pltpu.VMEM((rows, LANES), F32),
                            pltpu.VMEM((rows, LANES), F32),
                            pltpu.VMEM((LANES, LANES), F32),
                            pltpu.VMEM((nblk, rows, D_ATTN), F32)]),
        out_shape=jax.ShapeDtypeStruct((bsz * tq, D_ATTN), F32),
        compiler_params=_cparams(("parallel", "arbitrary")),
        name="moba_sample",
    )(page_table, slopes, q, kn, vn, *([ckt] * npage), *([cvt] * npage))


def _rglru_kernel(xr_ref, yr_ref, conv0_ref, h0_ref, wconv_ref, bconv_ref, wa_ref, ba_ref,
                  wx_ref, bx_ref, lam_ref, rec_ref, hlast_ref, convout_ref,
                  xpad_ref, a_ref, b_ref, hs_ref, h_ref, *, tc):
    c = pl.program_id(1)
    nc = pl.num_programs(1)
    pad = 8
    keep = CONV_W - 1

    @pl.when(c == 0)
    def _():
        xpad_ref[pad - keep:pad, :] = conv0_ref[0]
        h_ref[...] = h0_ref[0]

    xpad_ref[pad:pad + tc, :] = xr_ref[...]
    xc = bconv_ref[...] + wconv_ref[0:1, :] * xpad_ref[pad - 3:pad - 3 + tc, :]
    for jj in range(1, CONV_W):
        xc = xc + wconv_ref[jj:jj + 1, :] * xpad_ref[pad - 3 + jj:pad - 3 + jj + tc, :]
    tail = xpad_ref[pad + tc - keep:pad + tc, :]
    xpad_ref[pad - keep:pad, :] = tail
    convout_ref[0] = tail

    r = jax.nn.sigmoid(_dot(xc, wa_ref[...]) + ba_ref[...])
    gi = jax.nn.sigmoid(_dot(xc, wx_ref[...]) + bx_ref[...])
    lam = lam_ref[...]
    softplus_neg = jnp.maximum(-lam, 0.0) + jnp.log1p(jnp.exp(-jnp.abs(lam)))
    log_a = (-LRU_C) * r * softplus_neg
    a_ref[...] = jnp.exp(log_a)
    th = jnp.tanh(log_a)
    b_ref[...] = jnp.sqrt(-2.0 * th / (1.0 - th)) * (gi * xc)

    def step(t8, h):
        t0 = pl.multiple_of(t8 * 8, 8)
        a8 = a_ref[pl.ds(t0, 8), :]
        b8 = b_ref[pl.ds(t0, 8), :]
        out = []
        for rr in range(8):
            h = a8[rr:rr + 1, :] * h + b8[rr:rr + 1, :]
            out.append(h)
        hs_ref[pl.ds(t0, 8), :] = jnp.concatenate(out, axis=0)
        return h

    h = lax.fori_loop(0, tc // 8, step, h_ref[...])
    h_ref[...] = h
    rec_ref[...] = hs_ref[...] * _gelu(yr_ref[...])

    @pl.when(c == nc - 1)
    def _():
        hlast_ref[0] = h


def _rglru(xr, yr, conv0, h0, w_conv, b_conv, wa_bd, b_a, wx_bd, b_x, lam, bsz, t, tc):
    ncs = t // tc
    row = lambda b, c: (b * ncs + c, 0)
    per_b = lambda b, c: (b, 0, 0)
    fix = lambda b, c: (0, 0)
    kern = functools.partial(_rglru_kernel, tc=tc)
    vec = pl.BlockSpec((1, D_LRU), fix)
    return pl.pallas_call(
        kern,
        grid=(bsz, ncs),
        in_specs=[pl.BlockSpec((tc, D_LRU), row),
                  pl.BlockSpec((tc, D_LRU), row),
                  pl.BlockSpec((1, CONV_W - 1, D_LRU), per_b),
                  pl.BlockSpec((1, 1, D_LRU), per_b),
                  pl.BlockSpec((CONV_W, D_LRU), fix),
                  vec,
                  pl.BlockSpec((D_LRU, D_LRU), fix), vec,
                  pl.BlockSpec((D_LRU, D_LRU), fix), vec,
                  vec],
        out_specs=[pl.BlockSpec((tc, D_LRU), row),
                   pl.BlockSpec((1, 1, D_LRU), per_b),
                   pl.BlockSpec((1, CONV_W - 1, D_LRU), per_b)],
        out_shape=[jax.ShapeDtypeStruct((bsz * t, D_LRU), F32),
                   jax.ShapeDtypeStruct((bsz, 1, D_LRU), F32),
                   jax.ShapeDtypeStruct((bsz, CONV_W - 1, D_LRU), F32)],
        scratch_shapes=[pltpu.VMEM((tc + 8, D_LRU), F32),
                        pltpu.VMEM((tc, D_LRU), F32),
                        pltpu.VMEM((tc, D_LRU), F32),
                        pltpu.VMEM((tc, D_LRU), F32),
                        pltpu.VMEM((1, D_LRU), F32)],
        compiler_params=_cparams(("parallel", "arbitrary")),
        name="rglru",
    )(xr, yr, conv0, h0, w_conv, b_conv, wa_bd, b_a, wx_bd, b_x, lam)


def _merge_kernel(x_ref, attn_ref, rec_ref, ga_ref, gb_ref, woa_ref, wob_ref, wout_ref, gffn_ref,
                  x2_ref, xn_ref):
    merged = (jax.nn.sigmoid(ga_ref[...]) * _dot(attn_ref[...], woa_ref[...])
              + jax.nn.sigmoid(gb_ref[...]) * _dot(rec_ref[...], wob_ref[...]))
    x2 = x_ref[...] + _dot(merged, wout_ref[...])
    x2_ref[...] = x2
    ms = jnp.mean(x2 * x2, axis=-1, keepdims=True)
    xn_ref[...] = x2 * lax.rsqrt(ms + EPS) * gffn_ref[...]


def _merge(x, attn, rec, ga, gb, w_oa, w_ob, w_out, g_ffn, tm):
    n = x.shape[0]
    row = lambda i: (i, 0)
    fix = lambda i: (0, 0)
    return pl.pallas_call(
        _merge_kernel,
        grid=(n // tm,),
        in_specs=[pl.BlockSpec((tm, D_MODEL), row),
                  pl.BlockSpec((tm, D_ATTN), row),
                  pl.BlockSpec((tm, D_LRU), row),
                  pl.BlockSpec((tm, D_MODEL), row),
                  pl.BlockSpec((tm, D_MODEL), row),
                  pl.BlockSpec((D_ATTN, D_MODEL), fix),
                  pl.BlockSpec((D_LRU, D_MODEL), fix),
                  pl.BlockSpec((D_MODEL, D_MODEL), fix),
                  pl.BlockSpec((1, D_MODEL), fix)],
        out_specs=[pl.BlockSpec((tm, D_MODEL), row), pl.BlockSpec((tm, D_MODEL), row)],
        out_shape=[jax.ShapeDtypeStruct((n, D_MODEL), F32), jax.ShapeDtypeStruct((n, D_MODEL), F32)],
        compiler_params=_cparams(("parallel",)),
        name="merge",
    )(x, attn, rec, ga, gb, w_oa, w_ob, w_out, g_ffn)


_STAIR = [(i, PEER_TOPK // (i + 1)) for i in range(PEER_TOPK)]


def _route_kernel(xn_ref, wpq_ref, keys_ref, eid_ref, g_ref, *, tt):
    k = PEER_TOPK
    qt = _dot_nt(wpq_ref[...], xn_ref[...].astype(BF16))
    rio = lax.broadcasted_iota(I32, (PEER_NKEYS, tt), 0).astype(F32)

    def top_rows(cur, tie, payload, count, big):
        vals, pays = [], []
        for _ in range(count):
            m = jnp.max(cur, axis=0, keepdims=True)
            f = jnp.min(jnp.where(cur == m, tie, big), axis=0, keepdims=True)
            hit = tie == f
            vals.append(m)
            if payload is None:
                pays.append(f)
            else:
                pays.append(jnp.max(jnp.where(hit, payload, -1.0), axis=0, keepdims=True))
            cur = jnp.where(hit, NEG_INF, cur)
        return jnp.concatenate(vals, axis=0), jnp.concatenate(pays, axis=0)

    eids, gs = [], []
    for h in range(PEER_HEADS):
        sv, si = [], []
        for p in range(2):
            hp = h * 2 + p
            s = jnp.dot(keys_ref[hp], qt[hp * PEER_HALF:(hp + 1) * PEER_HALF, :].astype(BF16),
                        preferred_element_type=F32)
            v, ix = top_rows(s, rio, None, k, float(PEER_NKEYS))
            sv.append(v)
            si.append(ix)
        cands, cids, flats = [], [], []
        j8 = lax.broadcasted_iota(I32, (8, tt), 0)
        j8f = j8.astype(F32)
        cands.append(sv[0][0:1] + sv[1])
        cids.append(si[0][0:1] * PEER_NKEYS + si[1])
        flats.append(lax.broadcasted_iota(I32, (k, tt), 0).astype(F32))
        for i, nj in _STAIR[1:8]:
            cands.append(jnp.where(j8 < nj, sv[0][i:i + 1] + sv[1][0:8], NEG_INF))
            cids.append(si[0][i:i + 1] * PEER_NKEYS + si[1][0:8])
            flats.append(float(i * k) + j8f)
        cands.append(sv[0][8:16] + sv[1][0:1])
        cids.append(si[0][8:16] * PEER_NKEYS + si[1][0:1])
        flats.append((j8f + 8.0) * k)
        cand = jnp.concatenate(cands, axis=0)
        cid = jnp.concatenate(cids, axis=0)
        flat = jnp.concatenate(flats, axis=0)
        fv, eid = top_rows(cand, flat, cid, k, float(k * k))
        e = jnp.exp(fv - fv[0:1])
        gs.append(e / jnp.sum(e, axis=0, keepdims=True))
        eids.append(eid * SLAB)
    eid_ref[...] = jnp.concatenate(eids, axis=0).T.astype(I32)
    g_ref[...] = jnp.concatenate(gs, axis=0).T


def _route(xn, w_pq_t, keys, tt):
    n = xn.shape[0]
    kern = functools.partial(_route_kernel, tt=tt)
    return pl.pallas_call(
        kern,
        grid=(n // tt,),
        in_specs=[pl.BlockSpec((tt, D_MODEL), lambda i: (i, 0)),
                  pl.BlockSpec(w_pq_t.shape, lambda i: (0, 0)),
                  pl.BlockSpec(keys.shape, lambda i: (0, 0, 0))],
        out_specs=[pl.BlockSpec((tt, PEER_SEL), lambda i: (i, 0)),
                   pl.BlockSpec((tt, PEER_SEL), lambda i: (i, 0))],
        out_shape=[jax.ShapeDtypeStruct((n, PEER_SEL), I32),
                   jax.ShapeDtypeStruct((n, PEER_SEL), F32)],
        compiler_params=_cparams(("parallel",)),
        name="peer_route",
    )(xn, w_pq_t, keys)


GROWS = PEER_SEL * CHUNKS


def _gather_rows(off_ref, tab_ref, gs_ref, c):
    row_ref = off_ref.at[c]
    for kk in range(PEER_SEL):
        e0 = pl.multiple_of(row_ref[kk], SLAB)
        gs_ref[kk * SLAB:(kk + 1) * SLAB, :] = tab_ref[pl.ds(e0, SLAB), :]


U_TOKENS_PER_TRIP = 16
V_TOKENS_PER_TRIP = 8


def _pipelined_tokens(tt, per_trip, gather, compute, gs0_ref, gs1_ref):
    bufs = (gs0_ref, gs1_ref)
    gather(0, gs0_ref)

    def trip(t, carry):
        c0 = t * per_trip
        for u in range(per_trip):
            gather(jnp.minimum(c0 + u + 1, tt - 1), bufs[(u + 1) % 2])
            compute(c0 + u, bufs[u % 2])
        return carry

    lax.fori_loop(0, tt // per_trip, trip, 0)


def _peer_u_kernel(eid_ref, x8_ref, g_ref, tab_ref, fold_ref, w_ref, gs0_ref, gs1_ref, d_ref, *, tt):
    sub = lax.broadcasted_iota(I32, (2 * CHUNKS, GROWS), 0)
    col = lax.broadcasted_iota(I32, (2 * CHUNKS, GROWS), 1)
    diag = (col & (CHUNKS - 1)) == (sub & (CHUNKS - 1))

    def gather(c, gs_ref):
        _gather_rows(eid_ref, tab_ref, gs_ref, c)

    def compute(c, gs_ref):
        hi, lo = _split_bf16(x8_ref[pl.ds(pl.multiple_of(c * CHUNKS, CHUNKS), CHUNKS), :])
        xl = jnp.concatenate([hi, lo], axis=0)
        r = _dot_nt(xl, pltpu.bitcast(gs_ref[...], BF16))
        d_ref[pl.ds(c, 1), :] = jnp.sum(jnp.where(diag, r, 0.0), axis=0, keepdims=True)

    _pipelined_tokens(tt, U_TOKENS_PER_TRIP, gather, compute, gs0_ref, gs1_ref)
    hi, lo = _split_bf16(d_ref[...])
    act = (jnp.dot(hi, fold_ref[...], preferred_element_type=F32)
           + jnp.dot(lo, fold_ref[...], preferred_element_type=F32))
    w_ref[...] = g_ref[...] * _gelu(act)


def _peer_u(eid, xn8, g, tab, fold, tt):
    n = eid.shape[0]
    kern = functools.partial(_peer_u_kernel, tt=tt)
    return pl.pallas_call(
        kern,
        grid=(n // tt,),
        in_specs=[pl.BlockSpec((tt, PEER_SEL), lambda i: (i, 0), memory_space=pltpu.SMEM),
                  pl.BlockSpec((tt * CHUNKS, LANES), lambda i: (i, 0)),
                  pl.BlockSpec((tt, PEER_SEL), lambda i: (i, 0)),
                  pl.BlockSpec(memory_space=pltpu.VMEM),
                  pl.BlockSpec((GROWS, PEER_SEL), lambda i: (0, 0))],
        out_specs=pl.BlockSpec((tt, PEER_SEL), lambda i: (i, 0)),
        out_shape=jax.ShapeDtypeStruct((n, PEER_SEL), F32),
        scratch_shapes=[pltpu.VMEM((PEER_SEL * SLAB, LANES), I32),
                        pltpu.VMEM((PEER_SEL * SLAB, LANES), I32),
                        pltpu.VMEM((tt, GROWS), F32)],
        compiler_params=_cparams(("arbitrary",)),
        name="peer_u",
    )(eid, xn8, g, tab, fold)


def _peer_v_kernel(eid_ref, w_ref, x28_ref, tab_ref, spread_ref, y8_ref, gs0_ref, gs1_ref, w8_ref, *, tt):
    hi, lo = _split_bf16(w_ref[...])
    w8_ref[...] = (jnp.dot(hi, spread_ref[...], preferred_element_type=F32)
                   + jnp.dot(lo, spread_ref[...], preferred_element_type=F32))
    sub = lax.broadcasted_iota(I32, (CHUNKS, GROWS), 0)
    col = lax.broadcasted_iota(I32, (CHUNKS, GROWS), 1)
    diag = (col & (CHUNKS - 1)) == sub

    def gather(c, gs_ref):
        _gather_rows(eid_ref, tab_ref, gs_ref, c)

    def compute(c, gs_ref):
        hi, lo = _split_bf16(jnp.where(diag, w8_ref[pl.ds(c, 1), :], 0.0))
        lhs = jnp.concatenate([hi, lo], axis=0)
        o = jnp.dot(lhs, pltpu.bitcast(gs_ref[...], BF16), preferred_element_type=F32)
        r0 = pl.multiple_of(c * CHUNKS, CHUNKS)
        y8_ref[pl.ds(r0, CHUNKS), :] = x28_ref[pl.ds(r0, CHUNKS), :] + (o[0:CHUNKS] + o[CHUNKS:])

    _pipelined_tokens(tt, V_TOKENS_PER_TRIP, gather, compute, gs0_ref, gs1_ref)


def _peer_v(eid, w, x28, tab, spread, tt):
    n = eid.shape[0]
    kern = functools.partial(_peer_v_kernel, tt=tt)
    return pl.pallas_call(
        kern,
        grid=(n // tt,),
        in_specs=[pl.BlockSpec((tt, PEER_SEL), lambda i: (i, 0), memory_space=pltpu.SMEM),
                  pl.BlockSpec((tt, PEER_SEL), lambda i: (i, 0)),
                  pl.BlockSpec((tt * CHUNKS, LANES), lambda i: (i, 0)),
                  pl.BlockSpec(memory_space=pltpu.VMEM),
                  pl.BlockSpec((PEER_SEL, GROWS), lambda i: (0, 0))],
        out_specs=pl.BlockSpec((tt * CHUNKS, LANES), lambda i: (i, 0)),
        out_shape=jax.ShapeDtypeStruct((n * CHUNKS, LANES), F32),
        scratch_shapes=[pltpu.VMEM((PEER_SEL * SLAB, LANES), I32),
                        pltpu.VMEM((PEER_SEL * SLAB, LANES), I32),
                        pltpu.VMEM((tt, GROWS), F32)],
        compiler_params=_cparams(("arbitrary",)),
        name="peer_v",
    )(eid, w, x28, tab, spread)


def _pack_table(tab):
    e = tab.shape[0]
    bits = lax.bitcast_convert_type(tab.astype(BF16), jnp.uint16).astype(jnp.uint32)
    bits = bits.reshape(e, SLAB, 2, LANES)
    words = bits[:, :, 0, :] | (bits[:, :, 1, :] << 16)
    return lax.bitcast_convert_type(words, I32).reshape(e * SLAB, LANES)


def _tile(n, pref):
    t = pref
    while n % t:
        t //= 2
    return t


def _layer(x, conv0, h0, attend, wts, bsz, t):
    n = bsz * t
    xf = x.reshape(n, D_MODEL)
    q, k, v, xr, yr, ga, gb = _in_proj(xf, wts["g_mix"], wts["w_in"], wts["b_in"], wts["g_q"],
                                       wts["g_k"], wts["ones_bd"], _tile(n, 256))
    attn = attend(q, k, v)
    rec, h_last, conv_out = _rglru(xr, yr, conv0, h0.reshape(bsz, 1, D_LRU), wts["w_conv"],
                                   wts["b_conv"], wts["wa_bd"], wts["b_a"], wts["wx_bd"], wts["b_x"],
                                   wts["lam"], bsz, t, _tile(t, 512))
    x2, xn = _merge(xf, attn, rec, ga, gb, wts["w_oa"], wts["w_ob"], wts["w_out"], wts["g_ffn"],
                    _tile(n, 256))
    tt = _tile(n, 256)
    eid, g = _route(xn, wts["w_pq_t"], wts["keys"], tt)
    wgt = _peer_u(eid, xn.reshape(n * CHUNKS, LANES), g, wts["tab_u"], wts["fold"], tt)
    y = _peer_v(eid, wgt, x2.reshape(n * CHUNKS, LANES), wts["tab_v"], wts["fold"].T, tt)
    return (y.reshape(bsz, t, D_MODEL), k.reshape(bsz, t, N_HEADS, HEAD_DIM),
            v.reshape(bsz, t, N_HEADS, HEAD_DIM), h_last.reshape(bsz, D_LRU), conv_out)


def _block_diag(w):
    nb, bi, bo = w.shape
    eye = jnp.eye(nb, dtype=w.dtype)
    return (eye[:, None, :, None] * w[:, :, None, :]).reshape(nb * bi, nb * bo)


def kernel(x_prompt, x_sample, cache_k, cache_v, state_h, state_conv, page_table, g_mix, w_in, b_in,
           g_q, g_k, w_conv, b_conv, w_a, b_a, w_x, b_x, lam, w_oa, w_ob, w_out, g_ffn, w_pq,
           sub_keys, peer_u, peer_v):
    row = lambda a: a.reshape(1, -1)
    head_ones = jnp.ones((1, HEAD_DIM, HEAD_DIM), F32)
    wts = {
        "g_mix": row(g_mix), "w_in": w_in.astype(BF16), "b_in": row(b_in),
        "g_q": row(jnp.tile(g_q, N_HEADS)), "g_k": row(jnp.tile(g_k, N_HEADS)),
        "ones_bd": _block_diag(jnp.tile(head_ones, (N_HEADS, 1, 1))).astype(BF16),
        "w_conv": w_conv, "b_conv": row(b_conv),
        "wa_bd": _block_diag(w_a).astype(BF16), "b_a": row(b_a),
        "wx_bd": _block_diag(w_x).astype(BF16), "b_x": row(b_x), "lam": row(lam),
        "w_oa": w_oa.astype(BF16), "w_ob": w_ob.astype(BF16), "w_out": w_out.astype(BF16),
        "g_ffn": row(g_ffn), "w_pq_t": w_pq.T.astype(BF16),
        "keys": sub_keys.reshape(PEER_HEADS * 2, PEER_NKEYS, PEER_HALF).astype(BF16),
        "tab_u": _pack_table(peer_u), "tab_v": _pack_table(peer_v),
        "fold": jnp.repeat(jnp.eye(PEER_SEL, dtype=BF16), CHUNKS, axis=0),
    }
    slopes = jnp.asarray(np.array([2.0 ** (-8.0 * (h + 1) / N_HEADS) for h in range(N_HEADS)],
                                  dtype=np.float32))
    bp, tp, _ = x_prompt.shape
    bs, ts, _ = x_sample.shape

    def attend_prompt(q, k, v):
        return _moba_prompt(q, k, v, slopes, bp, tp)

    def attend_sample(q, k, v):
        return _moba_sample(q, k, v, cache_k, cache_v, page_table, slopes, bs, ts)

    conv00 = jnp.zeros((bp, CONV_W - 1, D_LRU), x_prompt.dtype)
    h00 = jnp.zeros((bp, D_LRU), state_h.dtype)
    y_p, k_p, v_p, h_p, c_p = _layer(x_prompt, conv00, h00, attend_prompt, wts, bp, tp)
    y_s, k_s, v_s, h_s, c_s = _layer(x_sample, state_conv, state_h, attend_sample, wts, bs, ts)
    return (y_p, y_s, k_p, v_p, h_p, c_p, k_s, v_s, h_s, c_s)
```

```python
import functools
import math

import jax
import jax.numpy as jnp
import numpy as np
from jax import lax
from jax.experimental import pallas as pl
from jax.experimental.pallas import tpu as pltpu

F32 = jnp.float32
BF16 = jnp.bfloat16
I32 = jnp.int32

D_MODEL = 1024
N_HEADS = 8
HEAD_DIM = 64
D_ATTN = N_HEADS * HEAD_DIM
MOBA_BLOCK = 256
MOBA_TOPK = 3
PAGE_SIZE = 128
D_LRU = 512
LRU_BLOCKS = 8
CONV_W = 4
LRU_C = 8.0
PEER_HEADS = 8
PEER_NKEYS = 128
PEER_HALF = 128
PEER_TOPK = 16
PEER_SEL = PEER_HEADS * PEER_TOPK
EPS = 1e-6
NEG_INF = float("-inf")

LANES = 128
SUBLANES = 8
CHUNKS = D_MODEL // LANES
SLAB = CHUNKS // 2
VMEM_LIMIT = 56 * 1024 * 1024


def _cparams(sem):
    return pltpu.CompilerParams(dimension_semantics=sem, vmem_limit_bytes=VMEM_LIMIT)


def _gelu(x):
    c = math.sqrt(2.0 / math.pi)
    return x * (0.5 * (1.0 + jnp.tanh(c * (x + 0.044715 * (x * x * x)))))


def _dot(a, b):
    return jnp.dot(a.astype(BF16), b.astype(BF16), preferred_element_type=F32)


def _dot_nt(a, b, precision=None):
    return lax.dot_general(a, b, (((1,), (1,)), ((), ())), precision=precision,
                           preferred_element_type=F32)


def _split_bf16(x):
    hi = x.astype(BF16)
    lo = (x - hi.astype(F32)).astype(BF16)
    return hi, lo


def _inproj_kernel(x_ref, gmix_ref, w_ref, b_ref, gq_ref, gk_ref, ones_ref,
                   q_ref, k_ref, v_ref, xr_ref, yr_ref, ga_ref, gb_ref):
    x = x_ref[...]
    ms = jnp.mean(x * x, axis=-1, keepdims=True)
    xn = x * lax.rsqrt(ms + EPS) * gmix_ref[...]
    proj = _dot(xn, w_ref[...]) + b_ref[...]

    def head_norm(t, g):
        hi, lo = _split_bf16(t * t)
        s = (jnp.dot(hi, ones_ref[...], preferred_element_type=F32)
             + jnp.dot(lo, ones_ref[...], preferred_element_type=F32))
        return t * lax.rsqrt(s * (1.0 / HEAD_DIM) + EPS) * g

    a = D_ATTN
    q_ref[...] = head_norm(proj[:, 0:a], gq_ref[...])
    k_ref[...] = head_norm(proj[:, a:2 * a], gk_ref[...])
    v_ref[...] = proj[:, 2 * a:3 * a]
    xr_ref[...] = proj[:, 3 * a:3 * a + D_LRU]
    yr_ref[...] = proj[:, 3 * a + D_LRU:3 * a + 2 * D_LRU]
    o = 3 * a + 2 * D_LRU
    ga_ref[...] = proj[:, o:o + D_MODEL]
    gb_ref[...] = proj[:, o + D_MODEL:o + 2 * D_MODEL]


def _in_proj(x, gmix, w_in, b_in, gq, gk, ones_bd, tm):
    n = x.shape[0]
    d_in = w_in.shape[1]
    row = lambda i: (i, 0)
    fix = lambda i: (0, 0)
    outs = [D_ATTN, D_ATTN, D_ATTN, D_LRU, D_LRU, D_MODEL, D_MODEL]
    return pl.pallas_call(
        _inproj_kernel,
        grid=(n // tm,),
        in_specs=[pl.BlockSpec((tm, D_MODEL), row),
                  pl.BlockSpec((1, D_MODEL), fix),
                  pl.BlockSpec((D_MODEL, d_in), fix),
                  pl.BlockSpec((1, d_in), fix),
                  pl.BlockSpec((1, D_ATTN), fix),
                  pl.BlockSpec((1, D_ATTN), fix),
                  pl.BlockSpec((D_ATTN, D_ATTN), fix)],
        out_specs=[pl.BlockSpec((tm, w), row) for w in outs],
        out_shape=[jax.ShapeDtypeStruct((n, w), F32) for w in outs],
        compiler_params=_cparams(("parallel",)),
        name="in_proj",
    )(x, gmix, w_in, b_in, gq, gk, ones_bd)


def _moba_prompt_kernel(slopes_ref, q_ref, k_ref, v_ref, o_ref,
                        kb_ref, vt_ref, kmean_ref, bias2_ref, biaso_ref, selb_ref, qs_ref, sa_ref, sb_ref,
                        *, nblk):
    hp = pl.program_id(1)
    i = pl.program_id(2)
    blk = MOBA_BLOCK
    rc = (lax.broadcasted_iota(I32, (blk, blk), 1)
          - lax.broadcasted_iota(I32, (blk, blk), 0)).astype(F32)

    @pl.when(i == 0)
    def _():
        for j in range(nblk):
            kj = k_ref[j * blk:(j + 1) * blk, :]
            kmean_ref[j:j + 1, :] = jnp.mean(kj, axis=0, keepdims=True)
            kb_ref[j * blk:(j + 1) * blk, :] = kj.astype(BF16)
            vt_ref[j // 2, :, (j % 2) * blk:(j % 2 + 1) * blk] = (
                v_ref[j * blk:(j + 1) * blk, :].T.astype(BF16))
        for hh in range(2):
            b = -slopes_ref[hp * 2 + hh] * rc
            cols = slice(hh * blk, (hh + 1) * blk)
            bias2_ref[0:blk, cols] = b
            bias2_ref[blk:2 * blk, cols] = b
            biaso_ref[:, cols] = jnp.where(rc >= 0.0, b, NEG_INF)

    q2 = q_ref[...]
    lane = lax.broadcasted_iota(I32, (blk, LANES), 1)
    jrow = lax.broadcasted_iota(I32, (nblk, blk), 0)
    for hh in range(2):
        qh = jnp.where((lane >= hh * HEAD_DIM) & (lane < (hh + 1) * HEAD_DIM), q2, 0.0)
        gate = _dot_nt(kmean_ref[...], qh, precision=lax.Precision.HIGHEST)
        cnt = jnp.zeros((nblk, blk), I32)
        for ii in range(nblk):
            gi = gate[ii:ii + 1, :]
            better = (gi > gate) | ((gi == gate) & (ii < jrow))
            cnt = cnt + jnp.where(better, jnp.where(ii < i, 1, 0), 0)
        selb_ref[:, hh * blk:(hh + 1) * blk] = jnp.where((cnt < MOBA_TOPK) & (jrow < i), 0.0, NEG_INF)
        qs_ref[hh * blk:(hh + 1) * blk, :] = (qh * (HEAD_DIM ** -0.5)).astype(BF16)

    col2 = lax.broadcasted_iota(I32, (1, 2 * blk), 1)
    slope_row = jnp.where(col2 < blk, slopes_ref[hp * 2], slopes_ref[hp * 2 + 1])

    def pv(vt, p, hh):
        return jnp.dot(vt[hh * HEAD_DIM:(hh + 1) * HEAD_DIM, :], p[:, hh * blk:(hh + 1) * blk],
                       preferred_element_type=F32)

    own0 = pl.multiple_of(i * blk, blk)
    s = _dot_nt(kb_ref[pl.ds(own0, blk), :], qs_ref[...]) + biaso_ref[...]
    m0 = jnp.max(s, axis=0, keepdims=True)
    p = jnp.exp(s - m0)
    l0 = jnp.sum(p, axis=0, keepdims=True)
    vto = v_ref[pl.ds(own0, blk), :].T.astype(BF16)
    pb = p.astype(BF16)

    npairs = nblk // 2

    def scores(jp):
        k0 = pl.multiple_of(jp * 2 * blk, 2 * blk)
        return _dot_nt(kb_ref[pl.ds(k0, 2 * blk), :], qs_ref[...])

    def softmax_pv(jp, s_ref, carry):
        m_old, l_old, acc0, acc1 = carry
        s2 = s_ref[...] + bias2_ref[...]
        sa, sb = s2[0:blk], s2[blk:2 * blk]
        ja = 2 * jp
        off_a = selb_ref[pl.ds(ja, 1), :] - slope_row * ((i - ja) * blk).astype(F32)
        off_b = selb_ref[pl.ds(ja + 1, 1), :] - slope_row * ((i - ja - 1) * blk).astype(F32)
        m_new = jnp.maximum(m_old, jnp.maximum(jnp.max(sa, axis=0, keepdims=True) + off_a,
                                               jnp.max(sb, axis=0, keepdims=True) + off_b))
        alpha = jnp.exp(m_old - m_new)
        pa = jnp.exp(sa - (m_new - off_a))
        pb2 = jnp.exp(sb - (m_new - off_b))
        l_new = (alpha * l_old + jnp.sum(pa, axis=0, keepdims=True)
                 + jnp.sum(pb2, axis=0, keepdims=True))
        pp = jnp.concatenate([pa, pb2], axis=0).astype(BF16)
        vt = vt_ref[jp]
        acc0 = alpha[:, 0:blk] * acc0 + pv(vt, pp, 0)
        acc1 = alpha[:, blk:2 * blk] * acc1 + pv(vt, pp, 1)
        return m_new, l_new, acc0, acc1

    ntrips = (i + 1) // 2
    sa_ref[...] = scores(0)

    def body(t, carry):
        sb_ref[...] = scores(2 * t + 1)
        carry = softmax_pv(2 * t, sa_ref, carry)
        sa_ref[...] = scores(jnp.minimum(2 * t + 2, npairs - 1))
        return softmax_pv(2 * t + 1, sb_ref, carry)

    carry = lax.fori_loop(0, ntrips // 2, body, (m0, l0, pv(vto, pb, 0), pv(vto, pb, 1)))
    _, l_fin, acc0, acc1 = lax.cond(ntrips % 2 == 1,
                                    lambda c: softmax_pv(ntrips - 1, sa_ref, c),
                                    lambda c: c, carry)
    o_ref[...] = jnp.concatenate([acc0 / l_fin[:, 0:blk], acc1 / l_fin[:, blk:2 * blk]], axis=0).T


def _moba_prompt(q, k, v, slopes, bsz, t):
    nblk = t // MOBA_BLOCK
    assert nblk % 2 == 0
    blk = MOBA_BLOCK
    kern = functools.partial(_moba_prompt_kernel, nblk=nblk)
    return pl.pallas_call(
        kern,
        grid_spec=pltpu.PrefetchScalarGridSpec(
            num_scalar_prefetch=1,
            grid=(bsz, N_HEADS // 2, nblk),
            in_specs=[pl.BlockSpec((blk, LANES), lambda b, hp, i, s: (b * nblk + i, hp)),
                      pl.BlockSpec((t, LANES), lambda b, hp, i, s: (b, hp)),
                      pl.BlockSpec((t, LANES), lambda b, hp, i, s: (b, hp))],
            out_specs=pl.BlockSpec((blk, LANES), lambda b, hp, i, s: (b * nblk + i, hp)),
            scratch_shapes=[pltpu.VMEM((t, LANES), BF16),
                            pltpu.VMEM((nblk // 2, LANES, 2 * blk), BF16),
                            pltpu.VMEM((nblk, LANES), F32),
                            pltpu.VMEM((2 * blk, 2 * blk), F32),
                            pltpu.VMEM((blk, 2 * blk), F32),
                            pltpu.VMEM((nblk, 2 * blk), F32),
                            pltpu.VMEM((2 * blk, LANES), BF16),
                            pltpu.VMEM((2 * blk, 2 * blk), F32),
                            pltpu.VMEM((2 * blk, 2 * blk), F32)]),
        out_shape=jax.ShapeDtypeStruct((bsz * t, D_ATTN), F32),
        compiler_params=_cparams(("parallel", "parallel", "arbitrary")),
        name="moba_prompt",
    )(slopes, q, k, v)


SAMPLE_BLOCKS_PER_STEP = 4


def _moba_sample_kernel(pt_ref, slopes_ref, q_ref, kn_ref, vn_ref, *refs, nblk, tq, past):
    npage = SAMPLE_BLOCKS_PER_STEP * (MOBA_BLOCK // PAGE_SIZE)
    k_refs, v_refs = refs[0:npage], refs[npage:2 * npage]
    o_ref, qb_ref, qbt_ref, base_ref, m_ref, l_ref, g_ref, oall_ref = refs[2 * npage:]
    js = pl.program_id(1)
    nsteps = nblk // SAMPLE_BLOCKS_PER_STEP
    rows = N_HEADS * tq
    blk = MOBA_BLOCK
    lane_d = lax.broadcasted_iota(I32, (rows, D_ATTN), 1)
    row_d = lax.broadcasted_iota(I32, (rows, D_ATTN), 0)
    own_head = lane_d // HEAD_DIM == row_d // tq
    lane_s = lax.broadcasted_iota(I32, (rows, LANES), 1)

    def row_consts(width):
        r = lax.broadcasted_iota(I32, (rows, width), 0)
        hrow = r // tq
        slope = jnp.zeros((rows, width), F32)
        for h in range(N_HEADS):
            slope = jnp.where(hrow == h, slopes_ref[h], slope)
        qpos = (past + (r - hrow * tq)).astype(F32)
        return slope, qpos

    @pl.when(js == 0)
    def _():
        qb = jnp.where(own_head, jnp.concatenate([q_ref[...]] * N_HEADS, axis=0), 0.0)
        qb_ref[...] = qb
        qbt_ref[:, 0:rows] = qb.T
        slope, qpos = row_consts(blk)
        rel = lax.broadcasted_iota(I32, (rows, blk), 1).astype(F32)
        base_ref[...] = -slope * (qpos - rel)
        m_ref[...] = jnp.zeros_like(m_ref)
        l_ref[...] = jnp.zeros_like(l_ref)
        g_ref[...] = jnp.zeros_like(g_ref)

    qb = qb_ref[...]
    qs = (qb * (HEAD_DIM ** -0.5)).astype(BF16)

    def past_block(j, k_pages, v_pages):
        kt = jnp.concatenate([r[0] for r in k_pages], axis=1)
        vt = jnp.concatenate([r[0] for r in v_pages], axis=1)
        s = jnp.dot(qs, kt.astype(BF16), preferred_element_type=F32) + base_ref[...]
        mj = jnp.max(s, axis=-1, keepdims=True)
        p = jnp.exp(s - mj)
        lj = jnp.sum(p, axis=-1, keepdims=True)
        oall_ref[j] = _dot_nt(p.astype(BF16), vt.astype(BF16))
        kcol = jnp.sum(kt, axis=1, keepdims=True) * (1.0 / blk)
        g_ref[pl.ds(j, 1), :] = jnp.sum(qbt_ref[...] * kcol, axis=0, keepdims=True)
        here = lane_s == j
        slope1, _ = row_consts(1)
        mj = mj + slope1 * (j * blk).astype(F32)
        m_ref[...] = jnp.where(here, mj, m_ref[...])
        l_ref[...] = jnp.where(here, lj, l_ref[...])

    @pl.when(js < nsteps)
    def _():
        per = MOBA_BLOCK // PAGE_SIZE
        for a in range(SAMPLE_BLOCKS_PER_STEP):
            past_block(js * SAMPLE_BLOCKS_PER_STEP + a,
                       k_refs[a * per:(a + 1) * per], v_refs[a * per:(a + 1) * per])

    @pl.when(js == nsteps)
    def _():
        slope, qpos = row_consts(tq)
        kpos = (past + lax.broadcasted_iota(I32, (rows, tq), 1)).astype(F32)
        dist = qpos - kpos
        s = _dot_nt(qs, kn_ref[...].astype(BF16)) - slope * dist
        s = jnp.where(dist >= 0.0, s, NEG_INF)
        m_o = jnp.max(s, axis=-1, keepdims=True)
        p = jnp.exp(s - m_o)
        l_o = jnp.sum(p, axis=-1, keepdims=True)
        o_o = _dot(p, vn_ref[...])
        gate = g_ref[...]
        jrow = lax.broadcasted_iota(I32, (LANES, LANES), 0)
        cnt = jnp.zeros((LANES, LANES), I32)
        for ii in range(nblk):
            gi = gate[ii:ii + 1, :]
            better = (gi > gate) | ((gi == gate) & (ii < jrow))
            cnt = cnt + jnp.where(better, 1, 0)
        sel_t = jnp.where((cnt < MOBA_TOPK) & (jrow < nblk), 1.0, 0.0)
        sel = sel_t.T[0:rows, :] > 0.0
        m_all = m_ref[...]
        m_tot = jnp.maximum(m_o, jnp.max(jnp.where(sel, m_all, NEG_INF), axis=-1, keepdims=True))
        w = jnp.where(sel, jnp.exp(m_all - m_tot), 0.0)
        w_o = jnp.exp(m_o - m_tot)
        den = jnp.sum(w * l_ref[...], axis=-1, keepdims=True) + w_o * l_o
        num = w_o * o_o
        for jj in range(nblk):
            num = num + w[:, jj:jj + 1] * oall_ref[jj]
        res = jnp.where(own_head, num / den, 0.0)
        out = res[0:tq]
        for h in range(1, N_HEADS):
            out = out + res[h * tq:(h + 1) * tq]
        o_ref[...] = out


def _moba_sample(q, kn, vn, cache_k, cache_v, page_table, slopes, bsz, tq):
    n_pages = page_table.shape[1]
    per = MOBA_BLOCK // PAGE_SIZE
    nblk = n_pages // per
    past = n_pages * PAGE_SIZE
    rows = N_HEADS * tq
    assert nblk <= LANES and rows <= LANES and nblk % SAMPLE_BLOCKS_PER_STEP == 0
    nsteps = nblk // SAMPLE_BLOCKS_PER_STEP
    npage = SAMPLE_BLOCKS_PER_STEP * per
    n_pool = cache_k.shape[0]
    ckt = cache_k.transpose(0, 2, 3, 1).reshape(n_pool, D_ATTN, PAGE_SIZE)
    cvt = cache_v.transpose(0, 2, 3, 1).reshape(n_pool, D_ATTN, PAGE_SIZE)
    new = pl.BlockSpec((tq, D_ATTN), lambda b, j, pt, s: (b, 0))

    def page(which):
        return pl.BlockSpec(
            (1, D_ATTN, PAGE_SIZE),
            lambda b, j, pt, s: (pt[b, npage * jnp.minimum(j, nsteps - 1) + which], 0, 0))

    pages = [page(w) for w in range(npage)]
    kern = functools.partial(_moba_sample_kernel, nblk=nblk, tq=tq, past=past)
    return pl.pallas_call(
        kern,
        grid_spec=pltpu.PrefetchScalarGridSpec(
            num_scalar_prefetch=2,
            grid=(bsz, nsteps + 1),
            in_specs=[new, new, new] + pages + pages,
            out_specs=pl.BlockSpec((tq, D_ATTN), lambda b, j, pt, s: (b, 0)),
            scratch_shapes=[pltpu.VMEM((rows, D_ATTN), F32),
                            pltpu.VMEM((D_ATTN, LANES), F32),
                            pltpu.VMEM((rows, MOBA_BLOCK), F32),
                            pltpu.VMEM((rows, LANES), F32),
                            pltpu.VMEM((rows, LANES), F32),
                            pltpu.VMEM((LANES, LANES), F32),
                            pltpu.VMEM((nblk, rows, D_ATTN), F32)]),
        out_shape=jax.ShapeDtypeStruct((bsz * tq, D_ATTN), F32),
        compiler_params=_cparams(("parallel", "arbitrary")),
        name="moba_sample",
    )(page_table, slopes, q, kn, vn, *([ckt] * npage), *([cvt] * npage))


def _rglru_kernel(xr_ref, yr_ref, conv0_ref, h0_ref, wconv_ref, bconv_ref, wa_ref, ba_ref,
                  wx_ref, bx_ref, lam_ref, rec_ref, hlast_ref, convout_ref,
                  xpad_ref, a_ref, b_ref, hs_ref, h_ref, *, tc):
    c = pl.program_id(1)
    nc = pl.num_programs(1)
    pad = 8
    keep = CONV_W - 1

    @pl.when(c == 0)
    def _():
        xpad_ref[pad - keep:pad, :] = conv0_ref[0]
        h_ref[...] = h0_ref[0]

    xpad_ref[pad:pad + tc, :] = xr_ref[...]
    xc = bconv_ref[...] + wconv_ref[0:1, :] * xpad_ref[pad - 3:pad - 3 + tc, :]
    for jj in range(1, CONV_W):
        xc = xc + wconv_ref[jj:jj + 1, :] * xpad_ref[pad - 3 + jj:pad - 3 + jj + tc, :]
    tail = xpad_ref[pad + tc - keep:pad + tc, :]
    xpad_ref[pad - keep:pad, :] = tail
    convout_ref[0] = tail

    r = jax.nn.sigmoid(_dot(xc, wa_ref[...]) + ba_ref[...])
    gi = jax.nn.sigmoid(_dot(xc, wx_ref[...]) + bx_ref[...])
    lam = lam_ref[...]
    softplus_neg = jnp.maximum(-lam, 0.0) + jnp.log1p(jnp.exp(-jnp.abs(lam)))
    log_a = (-LRU_C) * r * softplus_neg
    a_ref[...] = jnp.exp(log_a)
    th = jnp.tanh(log_a)
    b_ref[...] = jnp.sqrt(-2.0 * th / (1.0 - th)) * (gi * xc)

    def step(t8, h):
        t0 = pl.multiple_of(t8 * 8, 8)
        a8 = a_ref[pl.ds(t0, 8), :]
        b8 = b_ref[pl.ds(t0, 8), :]
        out = []
        for rr in range(8):
            h = a8[rr:rr + 1, :] * h + b8[rr:rr + 1, :]
            out.append(h)
        hs_ref[pl.ds(t0, 8), :] = jnp.concatenate(out, axis=0)
        return h

    h = lax.fori_loop(0, tc // 8, step, h_ref[...])
    h_ref[...] = h
    rec_ref[...] = hs_ref[...] * _gelu(yr_ref[...])

    @pl.when(c == nc - 1)
    def _():
        hlast_ref[0] = h


def _rglru(xr, yr, conv0, h0, w_conv, b_conv, wa_bd, b_a, wx_bd, b_x, lam, bsz, t, tc):
    ncs = t // tc
    row = lambda b, c: (b * ncs + c, 0)
    per_b = lambda b, c: (b, 0, 0)
    fix = lambda b, c: (0, 0)
    kern = functools.partial(_rglru_kernel, tc=tc)
    vec = pl.BlockSpec((1, D_LRU), fix)
    return pl.pallas_call(
        kern,
        grid=(bsz, ncs),
        in_specs=[pl.BlockSpec((tc, D_LRU), row),
                  pl.BlockSpec((tc, D_LRU), row),
                  pl.BlockSpec((1, CONV_W - 1, D_LRU), per_b),
                  pl.BlockSpec((1, 1, D_LRU), per_b),
                  pl.BlockSpec((CONV_W, D_LRU), fix),
                  vec,
                  pl.BlockSpec((D_LRU, D_LRU), fix), vec,
                  pl.BlockSpec((D_LRU, D_LRU), fix), vec,
                  vec],
        out_specs=[pl.BlockSpec((tc, D_LRU), row),
                   pl.BlockSpec((1, 1, D_LRU), per_b),
                   pl.BlockSpec((1, CONV_W - 1, D_LRU), per_b)],
        out_shape=[jax.ShapeDtypeStruct((bsz * t, D_LRU), F32),
                   jax.ShapeDtypeStruct((bsz, 1, D_LRU), F32),
                   jax.ShapeDtypeStruct((bsz, CONV_W - 1, D_LRU), F32)],
        scratch_shapes=[pltpu.VMEM((tc + 8, D_LRU), F32),
                        pltpu.VMEM((tc, D_LRU), F32),
                        pltpu.VMEM((tc, D_LRU), F32),
                        pltpu.VMEM((tc, D_LRU), F32),
                        pltpu.VMEM((1, D_LRU), F32)],
        compiler_params=_cparams(("parallel", "arbitrary")),
        name="rglru",
    )(xr, yr, conv0, h0, w_conv, b_conv, wa_bd, b_a, wx_bd, b_x, lam)


def _merge_kernel(x_ref, attn_ref, rec_ref, ga_ref, gb_ref, woa_ref, wob_ref, wout_ref, gffn_ref,
                  x2_ref, xn_ref):
    merged = (jax.nn.sigmoid(ga_ref[...]) * _dot(attn_ref[...], woa_ref[...])
              + jax.nn.sigmoid(gb_ref[...]) * _dot(rec_ref[...], wob_ref[...]))
    x2 = x_ref[...] + _dot(merged, wout_ref[...])
    x2_ref[...] = x2
    ms = jnp.mean(x2 * x2, axis=-1, keepdims=True)
    xn_ref[...] = x2 * lax.rsqrt(ms + EPS) * gffn_ref[...]


def _merge(x, attn, rec, ga, gb, w_oa, w_ob, w_out, g_ffn, tm):
    n = x.shape[0]
    row = lambda i: (i, 0)
    fix = lambda i: (0, 0)
    return pl.pallas_call(
        _merge_kernel,
        grid=(n // tm,),
        in_specs=[pl.BlockSpec((tm, D_MODEL), row),
                  pl.BlockSpec((tm, D_ATTN), row),
                  pl.BlockSpec((tm, D_LRU), row),
                  pl.BlockSpec((tm, D_MODEL), row),
                  pl.BlockSpec((tm, D_MODEL), row),
                  pl.BlockSpec((D_ATTN, D_MODEL), fix),
                  pl.BlockSpec((D_LRU, D_MODEL), fix),
                  pl.BlockSpec((D_MODEL, D_MODEL), fix),
                  pl.BlockSpec((1, D_MODEL), fix)],
        out_specs=[pl.BlockSpec((tm, D_MODEL), row), pl.BlockSpec((tm, D_MODEL), row)],
        out_shape=[jax.ShapeDtypeStruct((n, D_MODEL), F32), jax.ShapeDtypeStruct((n, D_MODEL), F32)],
        compiler_params=_cparams(("parallel",)),
        name="merge",
    )(x, attn, rec, ga, gb, w_oa, w_ob, w_out, g_ffn)


_STAIR = [(i, PEER_TOPK // (i + 1)) for i in range(PEER_TOPK)]


def _route_kernel(xn_ref, wpq_ref, keys_ref, eid_ref, g_ref, *, tt):
    k = PEER_TOPK
    qt = _dot_nt(wpq_ref[...], xn_ref[...].astype(BF16))
    rio = lax.broadcasted_iota(I32, (PEER_NKEYS, tt), 0).astype(F32)

    def top_rows(cur, tie, payload, count, big):
        vals, pays = [], []
        for _ in range(count):
            m = jnp.max(cur, axis=0, keepdims=True)
            f = jnp.min(jnp.where(cur == m, tie, big), axis=0, keepdims=True)
            hit = tie == f
            vals.append(m)
            if payload is None:
                pays.append(f)
            else:
                pays.append(jnp.max(jnp.where(hit, payload, -1.0), axis=0, keepdims=True))
            cur = jnp.where(hit, NEG_INF, cur)
        return jnp.concatenate(vals, axis=0), jnp.concatenate(pays, axis=0)

    eids, gs = [], []
    for h in range(PEER_HEADS):
        sv, si = [], []
        for p in range(2):
            hp = h * 2 + p
            s = jnp.dot(keys_ref[hp], qt[hp * PEER_HALF:(hp + 1) * PEER_HALF, :].astype(BF16),
                        preferred_element_type=F32)
            v, ix = top_rows(s, rio, None, k, float(PEER_NKEYS))
            sv.append(v)
            si.append(ix)
        cands, cids, flats = [], [], []
        j8 = lax.broadcasted_iota(I32, (8, tt), 0)
        j8f = j8.astype(F32)
        cands.append(sv[0][0:1] + sv[1])
        cids.append(si[0][0:1] * PEER_NKEYS + si[1])
        flats.append(lax.broadcasted_iota(I32, (k, tt), 0).astype(F32))
        for i, nj in _STAIR[1:8]:
            cands.append(jnp.where(j8 < nj, sv[0][i:i + 1] + sv[1][0:8], NEG_INF))
            cids.append(si[0][i:i + 1] * PEER_NKEYS + si[1][0:8])
            flats.append(float(i * k) + j8f)
        cands.append(sv[0][8:16] + sv[1][0:1])
        cids.append(si[0][8:16] * PEER_NKEYS + si[1][0:1])
        flats.append((j8f + 8.0) * k)
        cand = jnp.concatenate(cands, axis=0)
        cid = jnp.concatenate(cids, axis=0)
        flat = jnp.concatenate(flats, axis=0)
        fv, eid = top_rows(cand, flat, cid, k, float(k * k))
        e = jnp.exp(fv - fv[0:1])
        gs.append(e / jnp.sum(e, axis=0, keepdims=True))
        eids.append(eid * SLAB)
    eid_ref[...] = jnp.concatenate(eids, axis=0).T.astype(I32)
    g_ref[...] = jnp.concatenate(gs, axis=0).T


def _route(xn, w_pq_t, keys, tt):
    n = xn.shape[0]
    kern = functools.partial(_route_kernel, tt=tt)
    return pl.pallas_call(
        kern,
        grid=(n // tt,),
        in_specs=[pl.BlockSpec((tt, D_MODEL), lambda i: (i, 0)),
                  pl.BlockSpec(w_pq_t.shape, lambda i: (0, 0)),
                  pl.BlockSpec(keys.shape, lambda i: (0, 0, 0))],
        out_specs=[pl.BlockSpec((tt, PEER_SEL), lambda i: (i, 0)),
                   pl.BlockSpec((tt, PEER_SEL), lambda i: (i, 0))],
        out_shape=[jax.ShapeDtypeStruct((n, PEER_SEL), I32),
                   jax.ShapeDtypeStruct((n, PEER_SEL), F32)],
        compiler_params=_cparams(("parallel",)),
        name="peer_route",
    )(xn, w_pq_t, keys)


GROWS = PEER_SEL * CHUNKS


def _gather_rows(off_ref, tab_ref, gs_ref, c):
    row_ref = off_ref.at[c]
    for kk in range(PEER_SEL):
        e0 = pl.multiple_of(row_ref[kk], SLAB)
        gs_ref[kk * SLAB:(kk + 1) * SLAB, :] = tab_ref[pl.ds(e0, SLAB), :]


GATHER_PARTS = 4
PART_ROWS = GROWS // GATHER_PARTS


def _gather_part(row_ref, tab_ref, q):
    n = PEER_SEL // GATHER_PARTS
    slabs = [tab_ref[pl.ds(pl.multiple_of(row_ref[kk], SLAB), SLAB), :]
             for kk in range(q * n, (q + 1) * n)]
    return pltpu.bitcast(jnp.concatenate(slabs, axis=0), BF16)


U_TOKENS_PER_TRIP = 16
V_TOKENS_PER_TRIP = 8


def _pipelined_tokens(tt, per_trip, gather, compute, gs0_ref, gs1_ref):
    bufs = (gs0_ref, gs1_ref)
    gather(0, gs0_ref)

    def trip(t, carry):
        c0 = t * per_trip
        for u in range(per_trip):
            gather(jnp.minimum(c0 + u + 1, tt - 1), bufs[(u + 1) % 2])
            compute(c0 + u, bufs[u % 2])
        return carry

    lax.fori_loop(0, tt // per_trip, trip, 0)


def _row_to_chunks(x_ref, c):
    sub = lax.broadcasted_iota(I32, (CHUNKS, LANES), 0)
    row = x_ref[pl.ds(c, 1), :]
    out = jnp.zeros((CHUNKS, LANES), F32)
    for i in range(CHUNKS):
        out = jnp.where(sub == i, row[:, i * LANES:(i + 1) * LANES], out)
    return out


def _peer_u_kernel(eid_ref, x_ref, g_ref, tab_ref, fold_ref, w_ref, gs0_ref, gs1_ref, d_ref, *, tt):
    sub = lax.broadcasted_iota(I32, (2 * CHUNKS, GROWS), 0)
    col = lax.broadcasted_iota(I32, (2 * CHUNKS, GROWS), 1)
    diag = (col & (CHUNKS - 1)) == (sub & (CHUNKS - 1))

    def gather(c, gs_ref):
        _gather_rows(eid_ref, tab_ref, gs_ref, c)

    def compute(c, gs_ref):
        hi, lo = _split_bf16(_row_to_chunks(x_ref, c))
        xl = jnp.concatenate([hi, lo], axis=0)
        r = _dot_nt(xl, pltpu.bitcast(gs_ref[...], BF16))
        d_ref[pl.ds(c, 1), :] = jnp.sum(jnp.where(diag, r, 0.0), axis=0, keepdims=True)

    _pipelined_tokens(tt, U_TOKENS_PER_TRIP, gather, compute, gs0_ref, gs1_ref)
    hi, lo = _split_bf16(d_ref[...])
    act = (jnp.dot(hi, fold_ref[...], preferred_element_type=F32)
           + jnp.dot(lo, fold_ref[...], preferred_element_type=F32))
    w_ref[...] = g_ref[...] * _gelu(act)


def _peer_u(eid, xn, g, tab, fold, tt):
    n = eid.shape[0]
    kern = functools.partial(_peer_u_kernel, tt=tt)
    return pl.pallas_call(
        kern,
        grid=(n // tt,),
        in_specs=[pl.BlockSpec((tt, PEER_SEL), lambda i: (i, 0), memory_space=pltpu.SMEM),
                  pl.BlockSpec((tt, D_MODEL), lambda i: (i, 0)),
                  pl.BlockSpec((tt, PEER_SEL), lambda i: (i, 0)),
                  pl.BlockSpec(memory_space=pltpu.VMEM),
                  pl.BlockSpec((GROWS, PEER_SEL), lambda i: (0, 0))],
        out_specs=pl.BlockSpec((tt, PEER_SEL), lambda i: (i, 0)),
        out_shape=jax.ShapeDtypeStruct((n, PEER_SEL), F32),
        scratch_shapes=[pltpu.VMEM((PEER_SEL * SLAB, LANES), I32),
                        pltpu.VMEM((PEER_SEL * SLAB, LANES), I32),
                        pltpu.VMEM((tt, GROWS), F32)],
        compiler_params=_cparams(("arbitrary",)),
        name="peer_u",
    )(eid, xn, g, tab, fold)


def _peer_v_kernel(eid_ref, w_ref, x2_ref, tab_ref, spread_ref, y_ref, w8_ref, *, tt):
    hi, lo = _split_bf16(w_ref[...])
    w8_ref[...] = (jnp.dot(hi, spread_ref[...], preferred_element_type=F32)
                   + jnp.dot(lo, spread_ref[...], preferred_element_type=F32))
    sub = lax.broadcasted_iota(I32, (CHUNKS, GROWS), 0)
    col = lax.broadcasted_iota(I32, (CHUNKS, GROWS), 1)
    diag = (col & (CHUNKS - 1)) == sub

    def compute(c):
        row_ref = eid_ref.at[c]
        hi, lo = _split_bf16(jnp.where(diag, w8_ref[pl.ds(c, 1), :], 0.0))
        lhs = jnp.concatenate([hi, lo], axis=0)
        o = None
        for q in range(GATHER_PARTS):
            part = jnp.dot(lhs[:, q * PART_ROWS:(q + 1) * PART_ROWS],
                           _gather_part(row_ref, tab_ref, q), preferred_element_type=F32)
            o = part if o is None else o + part
        o = o[0:CHUNKS] + o[CHUNKS:]
        row = jnp.concatenate([o[i:i + 1, :] for i in range(CHUNKS)], axis=1)
        y_ref[pl.ds(c, 1), :] = x2_ref[pl.ds(c, 1), :] + row

    def trip(t, carry):
        for u in range(V_TOKENS_PER_TRIP):
            compute(t * V_TOKENS_PER_TRIP + u)
        return carry

    lax.fori_loop(0, tt // V_TOKENS_PER_TRIP, trip, 0)


def _peer_v(eid, w, x2, tab, spread, tt):
    n = eid.shape[0]
    kern = functools.partial(_peer_v_kernel, tt=tt)
    return pl.pallas_call(
        kern,
        grid=(n // tt,),
        in_specs=[pl.BlockSpec((tt, PEER_SEL), lambda i: (i, 0), memory_space=pltpu.SMEM),
                  pl.BlockSpec((tt, PEER_SEL), lambda i: (i, 0)),
                  pl.BlockSpec((tt, D_MODEL), lambda i: (i, 0)),
                  pl.BlockSpec(memory_space=pltpu.VMEM),
                  pl.BlockSpec((PEER_SEL, GROWS), lambda i: (0, 0))],
        out_specs=pl.BlockSpec((tt, D_MODEL), lambda i: (i, 0)),
        out_shape=jax.ShapeDtypeStruct((n, D_MODEL), F32),
        scratch_shapes=[pltpu.VMEM((tt, GROWS), F32)],
        compiler_params=_cparams(("arbitrary",)),
        name="peer_v",
    )(eid, w, x2, tab, spread)


def _pack_table(tab):
    e = tab.shape[0]
    bits = lax.bitcast_convert_type(tab.astype(BF16), jnp.uint16).astype(jnp.uint32)
    bits = bits.reshape(e, SLAB, 2, LANES)
    words = bits[:, :, 0, :] | (bits[:, :, 1, :] << 16)
    return lax.bitcast_convert_type(words, I32).reshape(e * SLAB, LANES)


def _tile(n, pref):
    t = pref
    while n % t:
        t //= 2
    return t


def _layer(x, conv0, h0, attend, wts, bsz, t):
    n = bsz * t
    xf = x.reshape(n, D_MODEL)
    q, k, v, xr, yr, ga, gb = _in_proj(xf, wts["g_mix"], wts["w_in"], wts["b_in"], wts["g_q"],
                                       wts["g_k"], wts["ones_bd"], _tile(n, 256))
    attn = attend(q, k, v)
    rec, h_last, conv_out = _rglru(xr, yr, conv0, h0.reshape(bsz, 1, D_LRU), wts["w_conv"],
                                   wts["b_conv"], wts["wa_bd"], wts["b_a"], wts["wx_bd"], wts["b_x"],
                                   wts["lam"], bsz, t, _tile(t, 512))
    x2, xn = _merge(xf, attn, rec, ga, gb, wts["w_oa"], wts["w_ob"], wts["w_out"], wts["g_ffn"],
                    _tile(n, 256))
    tt = _tile(n, 256)
    eid, g = _route(xn, wts["w_pq_t"], wts["keys"], tt)
    wgt = _peer_u(eid, xn, g, wts["tab_u"], wts["fold"], tt)
    y = _peer_v(eid, wgt, x2, wts["tab_v"], wts["fold"].T, tt)
    return (y.reshape(bsz, t, D_MODEL), k.reshape(bsz, t, N_HEADS, HEAD_DIM),
            v.reshape(bsz, t, N_HEADS, HEAD_DIM), h_last.reshape(bsz, D_LRU), conv_out)


def _block_diag(w):
    nb, bi, bo = w.shape
    eye = jnp.eye(nb, dtype=w.dtype)
    return (eye[:, None, :, None] * w[:, :, None, :]).reshape(nb * bi, nb * bo)


def kernel(x_prompt, x_sample, cache_k, cache_v, state_h, state_conv, page_table, g_mix, w_in, b_in,
           g_q, g_k, w_conv, b_conv, w_a, b_a, w_x, b_x, lam, w_oa, w_ob, w_out, g_ffn, w_pq,
           sub_keys, peer_u, peer_v):
    row = lambda a: a.reshape(1, -1)
    head_ones = jnp.ones((1, HEAD_DIM, HEAD_DIM), F32)
    wts = {
        "g_mix": row(g_mix), "w_in": w_in.astype(BF16), "b_in": row(b_in),
        "g_q": row(jnp.tile(g_q, N_HEADS)), "g_k": row(jnp.tile(g_k, N_HEADS)),
        "ones_bd": _block_diag(jnp.tile(head_ones, (N_HEADS, 1, 1))).astype(BF16),
        "w_conv": w_conv, "b_conv": row(b_conv),
        "wa_bd": _block_diag(w_a).astype(BF16), "b_a": row(b_a),
        "wx_bd": _block_diag(w_x).astype(BF16), "b_x": row(b_x), "lam": row(lam),
        "w_oa": w_oa.astype(BF16), "w_ob": w_ob.astype(BF16), "w_out": w_out.astype(BF16),
        "g_ffn": row(g_ffn), "w_pq_t": w_pq.T.astype(BF16),
        "keys": sub_keys.reshape(PEER_HEADS * 2, PEER_NKEYS, PEER_HALF).astype(BF16),
        "tab_u": _pack_table(peer_u), "tab_v": _pack_table(peer_v),
        "fold": jnp.repeat(jnp.eye(PEER_SEL, dtype=BF16), CHUNKS, axis=0),
    }
    slopes = jnp.asarray(np.array([2.0 ** (-8.0 * (h + 1) / N_HEADS) for h in range(N_HEADS)],
                                  dtype=np.float32))
    bp, tp, _ = x_prompt.shape
    bs, ts, _ = x_sample.shape

    def attend_prompt(q, k, v):
        return _moba_prompt(q, k, v, slopes, bp, tp)

    def attend_sample(q, k, v):
        return _moba_sample(q, k, v, cache_k, cache_v, page_table, slopes, bs, ts)

    conv00 = jnp.zeros((bp, CONV_W - 1, D_LRU), x_prompt.dtype)
    h00 = jnp.zeros((bp, D_LRU), state_h.dtype)
    y_p, k_p, v_p, h_p, c_p = _layer(x_prompt, conv00, h00, attend_prompt, wts, bp, tp)
    y_s, k_s, v_s, h_s, c_s = _layer(x_sample, state_conv, state_h, attend_sample, wts, bs, ts)
    return (y_p, y_s, k_p, v_p, h_p, c_p, k_s, v_s, h_s, c_s)
```

```python
import functools
import math

import jax
import jax.numpy as jnp
import numpy as np
from jax import lax
from jax.experimental import pallas as pl
from jax.experimental.pallas import tpu as pltpu

F32 = jnp.float32
BF16 = jnp.bfloat16
I32 = jnp.int32

D_MODEL = 1024
N_HEADS = 8
HEAD_DIM = 64
D_ATTN = N_HEADS * HEAD_DIM
MOBA_BLOCK = 256
MOBA_TOPK = 3
PAGE_SIZE = 128
D_LRU = 512
LRU_BLOCKS = 8
CONV_W = 4
LRU_C = 8.0
PEER_HEADS = 8
PEER_NKEYS = 128
PEER_HALF = 128
PEER_TOPK = 16
PEER_SEL = PEER_HEADS * PEER_TOPK
EPS = 1e-6
NEG_INF = float("-inf")

LANES = 128
SUBLANES = 8
CHUNKS = D_MODEL // LANES
SLAB = CHUNKS // 2
VMEM_LIMIT = 56 * 1024 * 1024


def _cparams(sem):
    return pltpu.CompilerParams(dimension_semantics=sem, vmem_limit_bytes=VMEM_LIMIT)


def _gelu(x):
    c = math.sqrt(2.0 / math.pi)
    return x * (0.5 * (1.0 + jnp.tanh(c * (x + 0.044715 * (x * x * x)))))


def _dot(a, b):
    return jnp.dot(a.astype(BF16), b.astype(BF16), preferred_element_type=F32)


def _dot_nt(a, b, precision=None):
    return lax.dot_general(a, b, (((1,), (1,)), ((), ())), precision=precision,
                           preferred_element_type=F32)


def _split_bf16(x):
    hi = x.astype(BF16)
    lo = (x - hi.astype(F32)).astype(BF16)
    return hi, lo


def _inproj_kernel(x_ref, gmix_ref, w_ref, b_ref, gq_ref, gk_ref, ones_ref,
                   q_ref, k_ref, v_ref, xr_ref, yr_ref, ga_ref, gb_ref):
    x = x_ref[...]
    ms = jnp.mean(x * x, axis=-1, keepdims=True)
    xn = x * lax.rsqrt(ms + EPS) * gmix_ref[...]
    proj = _dot(xn, w_ref[...]) + b_ref[...]

    def head_norm(t, g):
        hi, lo = _split_bf16(t * t)
        s = (jnp.dot(hi, ones_ref[...], preferred_element_type=F32)
             + jnp.dot(lo, ones_ref[...], preferred_element_type=F32))
        return t * lax.rsqrt(s * (1.0 / HEAD_DIM) + EPS) * g

    a = D_ATTN
    q_ref[...] = head_norm(proj[:, 0:a], gq_ref[...])
    k_ref[...] = head_norm(proj[:, a:2 * a], gk_ref[...])
    v_ref[...] = proj[:, 2 * a:3 * a]
    xr_ref[...] = proj[:, 3 * a:3 * a + D_LRU]
    yr_ref[...] = proj[:, 3 * a + D_LRU:3 * a + 2 * D_LRU]
    o = 3 * a + 2 * D_LRU
    ga_ref[...] = proj[:, o:o + D_MODEL]
    gb_ref[...] = proj[:, o + D_MODEL:o + 2 * D_MODEL]


def _in_proj(x, gmix, w_in, b_in, gq, gk, ones_bd, tm):
    n = x.shape[0]
    d_in = w_in.shape[1]
    row = lambda i: (i, 0)
    fix = lambda i: (0, 0)
    outs = [D_ATTN, D_ATTN, D_ATTN, D_LRU, D_LRU, D_MODEL, D_MODEL]
    return pl.pallas_call(
        _inproj_kernel,
        grid=(n // tm,),
        in_specs=[pl.BlockSpec((tm, D_MODEL), row),
                  pl.BlockSpec((1, D_MODEL), fix),
                  pl.BlockSpec((D_MODEL, d_in), fix),
                  pl.BlockSpec((1, d_in), fix),
                  pl.BlockSpec((1, D_ATTN), fix),
                  pl.BlockSpec((1, D_ATTN), fix),
                  pl.BlockSpec((D_ATTN, D_ATTN), fix)],
        out_specs=[pl.BlockSpec((tm, w), row) for w in outs],
        out_shape=[jax.ShapeDtypeStruct((n, w), F32) for w in outs],
        compiler_params=_cparams(("parallel",)),
        name="in_proj",
    )(x, gmix, w_in, b_in, gq, gk, ones_bd)


def _moba_prompt_kernel(slopes_ref, q_ref, k_ref, v_ref, o_ref,
                        kb_ref, vt_ref, kmean_ref, bias2_ref, biaso_ref, selb_ref, qs_ref, sa_ref, sb_ref,
                        *, nblk):
    hp = pl.program_id(1)
    i = pl.program_id(2)
    blk = MOBA_BLOCK
    rc = (lax.broadcasted_iota(I32, (blk, blk), 1)
          - lax.broadcasted_iota(I32, (blk, blk), 0)).astype(F32)

    @pl.when(i == 0)
    def _():
        for j in range(nblk):
            kj = k_ref[j * blk:(j + 1) * blk, :]
            kmean_ref[j:j + 1, :] = jnp.mean(kj, axis=0, keepdims=True)
            kb_ref[j * blk:(j + 1) * blk, :] = kj.astype(BF16)
            vt_ref[j // 2, :, (j % 2) * blk:(j % 2 + 1) * blk] = (
                v_ref[j * blk:(j + 1) * blk, :].T.astype(BF16))
        for hh in range(2):
            b = -slopes_ref[hp * 2 + hh] * rc
            cols = slice(hh * blk, (hh + 1) * blk)
            bias2_ref[0:blk, cols] = b
            bias2_ref[blk:2 * blk, cols] = b
            biaso_ref[:, cols] = jnp.where(rc >= 0.0, b, NEG_INF)

    q2 = q_ref[...]
    lane = lax.broadcasted_iota(I32, (blk, LANES), 1)
    jrow = lax.broadcasted_iota(I32, (nblk, blk), 0)
    for hh in range(2):
        qh = jnp.where((lane >= hh * HEAD_DIM) & (lane < (hh + 1) * HEAD_DIM), q2, 0.0)
        gate = _dot_nt(kmean_ref[...], qh, precision=lax.Precision.HIGHEST)
        cnt = jnp.zeros((nblk, blk), I32)
        for ii in range(nblk):
            gi = gate[ii:ii + 1, :]
            better = (gi > gate) | ((gi == gate) & (ii < jrow))
            cnt = cnt + jnp.where(better, jnp.where(ii < i, 1, 0), 0)
        selb_ref[:, hh * blk:(hh + 1) * blk] = jnp.where((cnt < MOBA_TOPK) & (jrow < i), 0.0, NEG_INF)
        qs_ref[hh * blk:(hh + 1) * blk, :] = (qh * (HEAD_DIM ** -0.5)).astype(BF16)

    col2 = lax.broadcasted_iota(I32, (1, 2 * blk), 1)
    slope_row = jnp.where(col2 < blk, slopes_ref[hp * 2], slopes_ref[hp * 2 + 1])

    def pv(vt, p, hh):
        return jnp.dot(vt[hh * HEAD_DIM:(hh + 1) * HEAD_DIM, :], p[:, hh * blk:(hh + 1) * blk],
                       preferred_element_type=F32)

    own0 = pl.multiple_of(i * blk, blk)
    s = _dot_nt(kb_ref[pl.ds(own0, blk), :], qs_ref[...]) + biaso_ref[...]
    m0 = jnp.max(s, axis=0, keepdims=True)
    p = jnp.exp(s - m0)
    l0 = jnp.sum(p, axis=0, keepdims=True)
    vto = v_ref[pl.ds(own0, blk), :].T.astype(BF16)
    pb = p.astype(BF16)

    npairs = nblk // 2

    def scores(jp):
        k0 = pl.multiple_of(jp * 2 * blk, 2 * blk)
        return _dot_nt(kb_ref[pl.ds(k0, 2 * blk), :], qs_ref[...])

    def softmax_pv(jp, s_ref, carry):
        m_old, l_old, acc0, acc1 = carry
        s2 = s_ref[...] + bias2_ref[...]
        sa, sb = s2[0:blk], s2[blk:2 * blk]
        ja = 2 * jp
        off_a = selb_ref[pl.ds(ja, 1), :] - slope_row * ((i - ja) * blk).astype(F32)
        off_b = selb_ref[pl.ds(ja + 1, 1), :] - slope_row * ((i - ja - 1) * blk).astype(F32)
        m_new = jnp.maximum(m_old, jnp.maximum(jnp.max(sa, axis=0, keepdims=True) + off_a,
                                               jnp.max(sb, axis=0, keepdims=True) + off_b))
        alpha = jnp.exp(m_old - m_new)
        pa = jnp.exp(sa - (m_new - off_a))
        pb2 = jnp.exp(sb - (m_new - off_b))
        l_new = (alpha * l_old + jnp.sum(pa, axis=0, keepdims=True)
                 + jnp.sum(pb2, axis=0, keepdims=True))
        pp = jnp.concatenate([pa, pb2], axis=0).astype(BF16)
        vt = vt_ref[jp]
        acc0 = alpha[:, 0:blk] * acc0 + pv(vt, pp, 0)
        acc1 = alpha[:, blk:2 * blk] * acc1 + pv(vt, pp, 1)
        return m_new, l_new, acc0, acc1

    ntrips = (i + 1) // 2
    sa_ref[...] = scores(0)

    def body(t, carry):
        sb_ref[...] = scores(2 * t + 1)
        carry = softmax_pv(2 * t, sa_ref, carry)
        sa_ref[...] = scores(jnp.minimum(2 * t + 2, npairs - 1))
        return softmax_pv(2 * t + 1, sb_ref, carry)

    carry = lax.fori_loop(0, ntrips // 2, body, (m0, l0, pv(vto, pb, 0), pv(vto, pb, 1)))
    _, l_fin, acc0, acc1 = lax.cond(ntrips % 2 == 1,
                                    lambda c: softmax_pv(ntrips - 1, sa_ref, c),
                                    lambda c: c, carry)
    o_ref[...] = jnp.concatenate([acc0 / l_fin[:, 0:blk], acc1 / l_fin[:, blk:2 * blk]], axis=0).T


def _moba_prompt(q, k, v, slopes, bsz, t):
    nblk = t // MOBA_BLOCK
    assert nblk % 2 == 0
    blk = MOBA_BLOCK
    kern = functools.partial(_moba_prompt_kernel, nblk=nblk)
    return pl.pallas_call(
        kern,
        grid_spec=pltpu.PrefetchScalarGridSpec(
            num_scalar_prefetch=1,
            grid=(bsz, N_HEADS // 2, nblk),
            in_specs=[pl.BlockSpec((blk, LANES), lambda b, hp, i, s: (b * nblk + i, hp)),
                      pl.BlockSpec((t, LANES), lambda b, hp, i, s: (b, hp)),
                      pl.BlockSpec((t, LANES), lambda b, hp, i, s: (b, hp))],
            out_specs=pl.BlockSpec((blk, LANES), lambda b, hp, i, s: (b * nblk + i, hp)),
            scratch_shapes=[pltpu.VMEM((t, LANES), BF16),
                            pltpu.VMEM((nblk // 2, LANES, 2 * blk), BF16),
                            pltpu.VMEM((nblk, LANES), F32),
                            pltpu.VMEM((2 * blk, 2 * blk), F32),
                            pltpu.VMEM((blk, 2 * blk), F32),
                            pltpu.VMEM((nblk, 2 * blk), F32),
                            pltpu.VMEM((2 * blk, LANES), BF16),
                            pltpu.VMEM((2 * blk, 2 * blk), F32),
                            pltpu.VMEM((2 * blk, 2 * blk), F32)]),
        out_shape=jax.ShapeDtypeStruct((bsz * t, D_ATTN), F32),
        compiler_params=_cparams(("parallel", "parallel", "arbitrary")),
        name="moba_prompt",
    )(slopes, q, k, v)


SAMPLE_BLOCKS_PER_STEP = 4


def _moba_sample_kernel(pt_ref, slopes_ref, q_ref, kn_ref, vn_ref, *refs, nblk, tq, past):
    npage = SAMPLE_BLOCKS_PER_STEP * (MOBA_BLOCK // PAGE_SIZE)
    k_refs, v_refs = refs[0:npage], refs[npage:2 * npage]
    o_ref, qb_ref, qbt_ref, base_ref, m_ref, l_ref, g_ref, oall_ref = refs[2 * npage:]
    js = pl.program_id(1)
    nsteps = nblk // SAMPLE_BLOCKS_PER_STEP
    rows = N_HEADS * tq
    blk = MOBA_BLOCK
    lane_d = lax.broadcasted_iota(I32, (rows, D_ATTN), 1)
    row_d = lax.broadcasted_iota(I32, (rows, D_ATTN), 0)
    own_head = lane_d // HEAD_DIM == row_d // tq
    lane_s = lax.broadcasted_iota(I32, (rows, LANES), 1)

    def row_consts(width):
        r = lax.broadcasted_iota(I32, (rows, width), 0)
        hrow = r // tq
        slope = jnp.zeros((rows, width), F32)
        for h in range(N_HEADS):
            slope = jnp.where(hrow == h, slopes_ref[h], slope)
        qpos = (past + (r - hrow * tq)).astype(F32)
        return slope, qpos

    @pl.when(js == 0)
    def _():
        qb = jnp.where(own_head, jnp.concatenate([q_ref[...]] * N_HEADS, axis=0), 0.0)
        qb_ref[...] = qb
        qbt_ref[:, 0:rows] = qb.T
        slope, qpos = row_consts(blk)
        rel = lax.broadcasted_iota(I32, (rows, blk), 1).astype(F32)
        base_ref[...] = -slope * (qpos - rel)
        m_ref[...] = jnp.zeros_like(m_ref)
        l_ref[...] = jnp.zeros_like(l_ref)
        g_ref[...] = jnp.zeros_like(g_ref)

    qb = qb_ref[...]
    qs = (qb * (HEAD_DIM ** -0.5)).astype(BF16)

    def past_block(j, k_pages, v_pages):
        kt = jnp.concatenate([r[0] for r in k_pages], axis=1)
        vt = jnp.concatenate([r[0] for r in v_pages], axis=1)
        s = jnp.dot(qs, kt.astype(BF16), preferred_element_type=F32) + base_ref[...]
        mj = jnp.max(s, axis=-1, keepdims=True)
        p = jnp.exp(s - mj)
        lj = jnp.sum(p, axis=-1, keepdims=True)
        oall_ref[j] = _dot_nt(p.astype(BF16), vt.astype(BF16))
        kcol = jnp.sum(kt, axis=1, keepdims=True) * (1.0 / blk)
        g_ref[pl.ds(j, 1), :] = jnp.sum(qbt_ref[...] * kcol, axis=0, keepdims=True)
        here = lane_s == j
        slope1, _ = row_consts(1)
        mj = mj + slope1 * (j * blk).astype(F32)
        m_ref[...] = jnp.where(here, mj, m_ref[...])
        l_ref[...] = jnp.where(here, lj, l_ref[...])

    @pl.when(js < nsteps)
    def _():
        per = MOBA_BLOCK // PAGE_SIZE
        for a in range(SAMPLE_BLOCKS_PER_STEP):
            past_block(js * SAMPLE_BLOCKS_PER_STEP + a,
                       k_refs[a * per:(a + 1) * per], v_refs[a * per:(a + 1) * per])

    @pl.when(js == nsteps)
    def _():
        slope, qpos = row_consts(tq)
        kpos = (past + lax.broadcasted_iota(I32, (rows, tq), 1)).astype(F32)
        dist = qpos - kpos
        s = _dot_nt(qs, kn_ref[...].astype(BF16)) - slope * dist
        s = jnp.where(dist >= 0.0, s, NEG_INF)
        m_o = jnp.max(s, axis=-1, keepdims=True)
        p = jnp.exp(s - m_o)
        l_o = jnp.sum(p, axis=-1, keepdims=True)
        o_o = _dot(p, vn_ref[...])
        gate = g_ref[...]
        jrow = lax.broadcasted_iota(I32, (LANES, LANES), 0)
        cnt = jnp.zeros((LANES, LANES), I32)
        for ii in range(nblk):
            gi = gate[ii:ii + 1, :]
            better = (gi > gate) | ((gi == gate) & (ii < jrow))
            cnt = cnt + jnp.where(better, 1, 0)
        sel_t = jnp.where((cnt < MOBA_TOPK) & (jrow < nblk), 1.0, 0.0)
        sel = sel_t.T[0:rows, :] > 0.0
        m_all = m_ref[...]
        m_tot = jnp.maximum(m_o, jnp.max(jnp.where(sel, m_all, NEG_INF), axis=-1, keepdims=True))
        w = jnp.where(sel, jnp.exp(m_all - m_tot), 0.0)
        w_o = jnp.exp(m_o - m_tot)
        den = jnp.sum(w * l_ref[...], axis=-1, keepdims=True) + w_o * l_o
        num = w_o * o_o
        for jj in range(nblk):
            num = num + w[:, jj:jj + 1] * oall_ref[jj]
        res = jnp.where(own_head, num / den, 0.0)
        out = res[0:tq]
        for h in range(1, N_HEADS):
            out = out + res[h * tq:(h + 1) * tq]
        o_ref[...] = out


def _moba_sample(q, kn, vn, cache_k, cache_v, page_table, slopes, bsz, tq):
    n_pages = page_table.shape[1]
    per = MOBA_BLOCK // PAGE_SIZE
    nblk = n_pages // per
    past = n_pages * PAGE_SIZE
    rows = N_HEADS * tq
    assert nblk <= LANES and rows <= LANES and nblk % SAMPLE_BLOCKS_PER_STEP == 0
    nsteps = nblk // SAMPLE_BLOCKS_PER_STEP
    npage = SAMPLE_BLOCKS_PER_STEP * per
    n_pool = cache_k.shape[0]
    ckt = cache_k.transpose(0, 2, 3, 1).reshape(n_pool, D_ATTN, PAGE_SIZE)
    cvt = cache_v.transpose(0, 2, 3, 1).reshape(n_pool, D_ATTN, PAGE_SIZE)
    new = pl.BlockSpec((tq, D_ATTN), lambda b, j, pt, s: (b, 0))

    def page(which):
        return pl.BlockSpec(
            (1, D_ATTN, PAGE_SIZE),
            lambda b, j, pt, s: (pt[b, npage * jnp.minimum(j, nsteps - 1) + which], 0, 0))

    pages = [page(w) for w in range(npage)]
    kern = functools.partial(_moba_sample_kernel, nblk=nblk, tq=tq, past=past)
    return pl.pallas_call(
        kern,
        grid_spec=pltpu.PrefetchScalarGridSpec(
            num_scalar_prefetch=2,
            grid=(bsz, nsteps + 1),
            in_specs=[new, new, new] + pages + pages,
            out_specs=pl.BlockSpec((tq, D_ATTN), lambda b, j, pt, s: (b, 0)),
            scratch_shapes=[pltpu.VMEM((rows, D_ATTN), F32),
                            pltpu.VMEM((D_ATTN, LANES), F32),
                            pltpu.VMEM((rows, MOBA_BLOCK), F32),
                            pltpu.VMEM((rows, LANES), F32),
                            pltpu.VMEM((rows, LANES), F32),
                            pltpu.VMEM((LANES, LANES), F32),
                            pltpu.VMEM((nblk, rows, D_ATTN), F32)]),
        out_shape=jax.ShapeDtypeStruct((bsz * tq, D_ATTN), F32),
        compiler_params=_cparams(("parallel", "arbitrary")),
        name="moba_sample",
    )(page_table, slopes, q, kn, vn, *([ckt] * npage), *([cvt] * npage))


def _rglru_kernel(xr_ref, yr_ref, conv0_ref, h0_ref, wconv_ref, bconv_ref, wa_ref, ba_ref,
                  wx_ref, bx_ref, lam_ref, rec_ref, hlast_ref, convout_ref,
                  xpad_ref, a_ref, b_ref, hs_ref, h_ref, *, tc):
    c = pl.program_id(1)
    nc = pl.num_programs(1)
    pad = 8
    keep = CONV_W - 1

    @pl.when(c == 0)
    def _():
        xpad_ref[pad - keep:pad, :] = conv0_ref[0]
        h_ref[...] = h0_ref[0]

    xpad_ref[pad:pad + tc, :] = xr_ref[...]
    xc = bconv_ref[...] + wconv_ref[0:1, :] * xpad_ref[pad - 3:pad - 3 + tc, :]
    for jj in range(1, CONV_W):
        xc = xc + wconv_ref[jj:jj + 1, :] * xpad_ref[pad - 3 + jj:pad - 3 + jj + tc, :]
    tail = xpad_ref[pad + tc - keep:pad + tc, :]
    xpad_ref[pad - keep:pad, :] = tail
    convout_ref[0] = tail

    r = jax.nn.sigmoid(_dot(xc, wa_ref[...]) + ba_ref[...])
    gi = jax.nn.sigmoid(_dot(xc, wx_ref[...]) + bx_ref[...])
    lam = lam_ref[...]
    softplus_neg = jnp.maximum(-lam, 0.0) + jnp.log1p(jnp.exp(-jnp.abs(lam)))
    log_a = (-LRU_C) * r * softplus_neg
    a_ref[...] = jnp.exp(log_a)
    th = jnp.tanh(log_a)
    b_ref[...] = jnp.sqrt(-2.0 * th / (1.0 - th)) * (gi * xc)

    def step(t8, h):
        t0 = pl.multiple_of(t8 * 8, 8)
        a8 = a_ref[pl.ds(t0, 8), :]
        b8 = b_ref[pl.ds(t0, 8), :]
        out = []
        for rr in range(8):
            h = a8[rr:rr + 1, :] * h + b8[rr:rr + 1, :]
            out.append(h)
        hs_ref[pl.ds(t0, 8), :] = jnp.concatenate(out, axis=0)
        return h

    h = lax.fori_loop(0, tc // 8, step, h_ref[...])
    h_ref[...] = h
    rec_ref[...] = hs_ref[...] * _gelu(yr_ref[...])

    @pl.when(c == nc - 1)
    def _():
        hlast_ref[0] = h


def _rglru(xr, yr, conv0, h0, w_conv, b_conv, wa_bd, b_a, wx_bd, b_x, lam, bsz, t, tc):
    ncs = t // tc
    row = lambda b, c: (b * ncs + c, 0)
    per_b = lambda b, c: (b, 0, 0)
    fix = lambda b, c: (0, 0)
    kern = functools.partial(_rglru_kernel, tc=tc)
    vec = pl.BlockSpec((1, D_LRU), fix)
    return pl.pallas_call(
        kern,
        grid=(bsz, ncs),
        in_specs=[pl.BlockSpec((tc, D_LRU), row),
                  pl.BlockSpec((tc, D_LRU), row),
                  pl.BlockSpec((1, CONV_W - 1, D_LRU), per_b),
                  pl.BlockSpec((1, 1, D_LRU), per_b),
                  pl.BlockSpec((CONV_W, D_LRU), fix),
                  vec,
                  pl.BlockSpec((D_LRU, D_LRU), fix), vec,
                  pl.BlockSpec((D_LRU, D_LRU), fix), vec,
                  vec],
        out_specs=[pl.BlockSpec((tc, D_LRU), row),
                   pl.BlockSpec((1, 1, D_LRU), per_b),
                   pl.BlockSpec((1, CONV_W - 1, D_LRU), per_b)],
        out_shape=[jax.ShapeDtypeStruct((bsz * t, D_LRU), F32),
                   jax.ShapeDtypeStruct((bsz, 1, D_LRU), F32),
                   jax.ShapeDtypeStruct((bsz, CONV_W - 1, D_LRU), F32)],
        scratch_shapes=[pltpu.VMEM((tc + 8, D_LRU), F32),
                        pltpu.VMEM((tc, D_LRU), F32),
                        pltpu.VMEM((tc, D_LRU), F32),
                        pltpu.VMEM((tc, D_LRU), F32),
                        pltpu.VMEM((1, D_LRU), F32)],
        compiler_params=_cparams(("parallel", "arbitrary")),
        name="rglru",
    )(xr, yr, conv0, h0, w_conv, b_conv, wa_bd, b_a, wx_bd, b_x, lam)


def _merge_kernel(x_ref, attn_ref, rec_ref, ga_ref, gb_ref, woa_ref, wob_ref, wout_ref, gffn_ref,
                  x2_ref, xn_ref):
    merged = (jax.nn.sigmoid(ga_ref[...]) * _dot(attn_ref[...], woa_ref[...])
              + jax.nn.sigmoid(gb_ref[...]) * _dot(rec_ref[...], wob_ref[...]))
    x2 = x_ref[...] + _dot(merged, wout_ref[...])
    x2_ref[...] = x2
    ms = jnp.mean(x2 * x2, axis=-1, keepdims=True)
    xn_ref[...] = x2 * lax.rsqrt(ms + EPS) * gffn_ref[...]


def _merge(x, attn, rec, ga, gb, w_oa, w_ob, w_out, g_ffn, tm):
    n = x.shape[0]
    row = lambda i: (i, 0)
    fix = lambda i: (0, 0)
    return pl.pallas_call(
        _merge_kernel,
        grid=(n // tm,),
        in_specs=[pl.BlockSpec((tm, D_MODEL), row),
                  pl.BlockSpec((tm, D_ATTN), row),
                  pl.BlockSpec((tm, D_LRU), row),
                  pl.BlockSpec((tm, D_MODEL), row),
                  pl.BlockSpec((tm, D_MODEL), row),
                  pl.BlockSpec((D_ATTN, D_MODEL), fix),
                  pl.BlockSpec((D_LRU, D_MODEL), fix),
                  pl.BlockSpec((D_MODEL, D_MODEL), fix),
                  pl.BlockSpec((1, D_MODEL), fix)],
        out_specs=[pl.BlockSpec((tm, D_MODEL), row), pl.BlockSpec((tm, D_MODEL), row)],
        out_shape=[jax.ShapeDtypeStruct((n, D_MODEL), F32), jax.ShapeDtypeStruct((n, D_MODEL), F32)],
        compiler_params=_cparams(("parallel",)),
        name="merge",
    )(x, attn, rec, ga, gb, w_oa, w_ob, w_out, g_ffn)


_STAIR = [(i, PEER_TOPK // (i + 1)) for i in range(PEER_TOPK)]


def _route_kernel(xn_ref, wpq_ref, keys_ref, eid_ref, g_ref, *, tt):
    k = PEER_TOPK
    qt = _dot_nt(wpq_ref[...], xn_ref[...].astype(BF16))
    rio = lax.broadcasted_iota(I32, (PEER_NKEYS, tt), 0).astype(F32)

    def top_rows(cur, tie, payload, count, big):
        vals, pays = [], []
        for _ in range(count):
            m = jnp.max(cur, axis=0, keepdims=True)
            f = jnp.min(jnp.where(cur == m, tie, big), axis=0, keepdims=True)
            hit = tie == f
            vals.append(m)
            if payload is None:
                pays.append(f)
            else:
                pays.append(jnp.max(jnp.where(hit, payload, -1.0), axis=0, keepdims=True))
            cur = jnp.where(hit, NEG_INF, cur)
        return jnp.concatenate(vals, axis=0), jnp.concatenate(pays, axis=0)

    eids, gs = [], []
    for h in range(PEER_HEADS):
        sv, si = [], []
        for p in range(2):
            hp = h * 2 + p
            s = jnp.dot(keys_ref[hp], qt[hp * PEER_HALF:(hp + 1) * PEER_HALF, :].astype(BF16),
                        preferred_element_type=F32)
            v, ix = top_rows(s, rio, None, k, float(PEER_NKEYS))
            sv.append(v)
            si.append(ix)
        cands, cids, flats = [], [], []
        j8 = lax.broadcasted_iota(I32, (8, tt), 0)
        j8f = j8.astype(F32)
        cands.append(sv[0][0:1] + sv[1])
        cids.append(si[0][0:1] * PEER_NKEYS + si[1])
        flats.append(lax.broadcasted_iota(I32, (k, tt), 0).astype(F32))
        for i, nj in _STAIR[1:8]:
            cands.append(jnp.where(j8 < nj, sv[0][i:i + 1] + sv[1][0:8], NEG_INF))
            cids.append(si[0][i:i + 1] * PEER_NKEYS + si[1][0:8])
            flats.append(float(i * k) + j8f)
        cands.append(sv[0][8:16] + sv[1][0:1])
        cids.append(si[0][8:16] * PEER_NKEYS + si[1][0:1])
        flats.append((j8f + 8.0) * k)
        cand = jnp.concatenate(cands, axis=0)
        cid = jnp.concatenate(cids, axis=0)
        flat = jnp.concatenate(flats, axis=0)
        fv, eid = top_rows(cand, flat, cid, k, float(k * k))
        e = jnp.exp(fv - fv[0:1])
        gs.append(e / jnp.sum(e, axis=0, keepdims=True))
        eids.append(eid * SLAB)
    eid_ref[...] = jnp.concatenate(eids, axis=0).T.astype(I32)
    g_ref[...] = jnp.concatenate(gs, axis=0).T


def _route(xn, w_pq_t, keys, tt):
    n = xn.shape[0]
    kern = functools.partial(_route_kernel, tt=tt)
    return pl.pallas_call(
        kern,
        grid=(n // tt,),
        in_specs=[pl.BlockSpec((tt, D_MODEL), lambda i: (i, 0)),
                  pl.BlockSpec(w_pq_t.shape, lambda i: (0, 0)),
                  pl.BlockSpec(keys.shape, lambda i: (0, 0, 0))],
        out_specs=[pl.BlockSpec((tt, PEER_SEL), lambda i: (i, 0)),
                   pl.BlockSpec((tt, PEER_SEL), lambda i: (i, 0))],
        out_shape=[jax.ShapeDtypeStruct((n, PEER_SEL), I32),
                   jax.ShapeDtypeStruct((n, PEER_SEL), F32)],
        compiler_params=_cparams(("parallel",)),
        name="peer_route",
    )(xn, w_pq_t, keys)


GROWS = PEER_SEL * CHUNKS


GATHER_PARTS = 4
PART_ROWS = GROWS // GATHER_PARTS
TOKENS_PER_TRIP = 8


def _gather_part(row_ref, tab_ref, q):
    n = PEER_SEL // GATHER_PARTS
    slabs = [tab_ref[pl.ds(pl.multiple_of(row_ref[kk], SLAB), SLAB), :]
             for kk in range(q * n, (q + 1) * n)]
    return pltpu.bitcast(jnp.concatenate(slabs, axis=0), BF16)


def _row_to_chunks(x_ref, c):
    sub = lax.broadcasted_iota(I32, (CHUNKS, LANES), 0)
    row = x_ref[pl.ds(c, 1), :]
    out = jnp.zeros((CHUNKS, LANES), F32)
    for i in range(CHUNKS):
        out = jnp.where(sub == i, row[:, i * LANES:(i + 1) * LANES], out)
    return out


def _peer_u_kernel(eid_ref, x_ref, g_ref, tab_ref, fold_ref, w_ref, d_ref, *, tt):
    sub = lax.broadcasted_iota(I32, (2 * CHUNKS, PART_ROWS), 0)
    col = lax.broadcasted_iota(I32, (2 * CHUNKS, PART_ROWS), 1)
    diag = (col & (CHUNKS - 1)) == (sub & (CHUNKS - 1))

    def compute(c):
        row_ref = eid_ref.at[c]
        hi, lo = _split_bf16(_row_to_chunks(x_ref, c))
        xl = jnp.concatenate([hi, lo], axis=0)
        parts = []
        for q in range(GATHER_PARTS):
            r = _dot_nt(xl, _gather_part(row_ref, tab_ref, q))
            parts.append(jnp.sum(jnp.where(diag, r, 0.0), axis=0, keepdims=True))
        d_ref[pl.ds(c, 1), :] = jnp.concatenate(parts, axis=1)

    def trip(t, carry):
        for u in range(TOKENS_PER_TRIP):
            compute(t * TOKENS_PER_TRIP + u)
        return carry

    lax.fori_loop(0, tt // TOKENS_PER_TRIP, trip, 0)
    hi, lo = _split_bf16(d_ref[...])
    act = (jnp.dot(hi, fold_ref[...], preferred_element_type=F32)
           + jnp.dot(lo, fold_ref[...], preferred_element_type=F32))
    w_ref[...] = g_ref[...] * _gelu(act)


def _peer_u(eid, xn, g, tab, fold, tt):
    n = eid.shape[0]
    kern = functools.partial(_peer_u_kernel, tt=tt)
    return pl.pallas_call(
        kern,
        grid=(n // tt,),
        in_specs=[pl.BlockSpec((tt, PEER_SEL), lambda i: (i, 0), memory_space=pltpu.SMEM),
                  pl.BlockSpec((tt, D_MODEL), lambda i: (i, 0)),
                  pl.BlockSpec((tt, PEER_SEL), lambda i: (i, 0)),
                  pl.BlockSpec(memory_space=pltpu.VMEM),
                  pl.BlockSpec((GROWS, PEER_SEL), lambda i: (0, 0))],
        out_specs=pl.BlockSpec((tt, PEER_SEL), lambda i: (i, 0)),
        out_shape=jax.ShapeDtypeStruct((n, PEER_SEL), F32),
        scratch_shapes=[pltpu.VMEM((tt, GROWS), F32)],
        compiler_params=_cparams(("arbitrary",)),
        name="peer_u",
    )(eid, xn, g, tab, fold)


def _peer_v_kernel(eid_ref, w_ref, x2_ref, tab_ref, spread_ref, y_ref, w8_ref, *, tt):
    hi, lo = _split_bf16(w_ref[...])
    w8_ref[...] = (jnp.dot(hi, spread_ref[...], preferred_element_type=F32)
                   + jnp.dot(lo, spread_ref[...], preferred_element_type=F32))
    sub = lax.broadcasted_iota(I32, (CHUNKS, GROWS), 0)
    col = lax.broadcasted_iota(I32, (CHUNKS, GROWS), 1)
    diag = (col & (CHUNKS - 1)) == sub

    def compute(c):
        row_ref = eid_ref.at[c]
        hi, lo = _split_bf16(jnp.where(diag, w8_ref[pl.ds(c, 1), :], 0.0))
        lhs = jnp.concatenate([hi, lo], axis=0)
        o = None
        for q in range(GATHER_PARTS):
            part = jnp.dot(lhs[:, q * PART_ROWS:(q + 1) * PART_ROWS],
                           _gather_part(row_ref, tab_ref, q), preferred_element_type=F32)
            o = part if o is None else o + part
        o = o[0:CHUNKS] + o[CHUNKS:]
        row = jnp.concatenate([o[i:i + 1, :] for i in range(CHUNKS)], axis=1)
        y_ref[pl.ds(c, 1), :] = x2_ref[pl.ds(c, 1), :] + row

    def trip(t, carry):
        for u in range(TOKENS_PER_TRIP):
            compute(t * TOKENS_PER_TRIP + u)
        return carry

    lax.fori_loop(0, tt // TOKENS_PER_TRIP, trip, 0)


def _peer_v(eid, w, x2, tab, spread, tt):
    n = eid.shape[0]
    kern = functools.partial(_peer_v_kernel, tt=tt)
    return pl.pallas_call(
        kern,
        grid=(n // tt,),
        in_specs=[pl.BlockSpec((tt, PEER_SEL), lambda i: (i, 0), memory_space=pltpu.SMEM),
                  pl.BlockSpec((tt, PEER_SEL), lambda i: (i, 0)),
                  pl.BlockSpec((tt, D_MODEL), lambda i: (i, 0)),
                  pl.BlockSpec(memory_space=pltpu.VMEM),
                  pl.BlockSpec((PEER_SEL, GROWS), lambda i: (0, 0))],
        out_specs=pl.BlockSpec((tt, D_MODEL), lambda i: (i, 0)),
        out_shape=jax.ShapeDtypeStruct((n, D_MODEL), F32),
        scratch_shapes=[pltpu.VMEM((tt, GROWS), F32)],
        compiler_params=_cparams(("arbitrary",)),
        name="peer_v",
    )(eid, w, x2, tab, spread)


def _pack_table(tab):
    e = tab.shape[0]
    bits = lax.bitcast_convert_type(tab.astype(BF16), jnp.uint16).astype(jnp.uint32)
    bits = bits.reshape(e, SLAB, 2, LANES)
    words = bits[:, :, 0, :] | (bits[:, :, 1, :] << 16)
    return lax.bitcast_convert_type(words, I32).reshape(e * SLAB, LANES)


def _tile(n, pref):
    t = pref
    while n % t:
        t //= 2
    return t


def _layer(x, conv0, h0, attend, wts, bsz, t):
    n = bsz * t
    xf = x.reshape(n, D_MODEL)
    q, k, v, xr, yr, ga, gb = _in_proj(xf, wts["g_mix"], wts["w_in"], wts["b_in"], wts["g_q"],
                                       wts["g_k"], wts["ones_bd"], _tile(n, 256))
    attn = attend(q, k, v)
    rec, h_last, conv_out = _rglru(xr, yr, conv0, h0.reshape(bsz, 1, D_LRU), wts["w_conv"],
                                   wts["b_conv"], wts["wa_bd"], wts["b_a"], wts["wx_bd"], wts["b_x"],
                                   wts["lam"], bsz, t, _tile(t, 512))
    x2, xn = _merge(xf, attn, rec, ga, gb, wts["w_oa"], wts["w_ob"], wts["w_out"], wts["g_ffn"],
                    _tile(n, 256))
    tt = _tile(n, 256)
    eid, g = _route(xn, wts["w_pq_t"], wts["keys"], tt)
    wgt = _peer_u(eid, xn, g, wts["tab_u"], wts["fold"], tt)
    y = _peer_v(eid, wgt, x2, wts["tab_v"], wts["fold"].T, tt)
    return (y.reshape(bsz, t, D_MODEL), k.reshape(bsz, t, N_HEADS, HEAD_DIM),
            v.reshape(bsz, t, N_HEADS, HEAD_DIM), h_last.reshape(bsz, D_LRU), conv_out)


def _block_diag(w):
    nb, bi, bo = w.shape
    eye = jnp.eye(nb, dtype=w.dtype)
    return (eye[:, None, :, None] * w[:, :, None, :]).reshape(nb * bi, nb * bo)


def kernel(x_prompt, x_sample, cache_k, cache_v, state_h, state_conv, page_table, g_mix, w_in, b_in,
           g_q, g_k, w_conv, b_conv, w_a, b_a, w_x, b_x, lam, w_oa, w_ob, w_out, g_ffn, w_pq,
           sub_keys, peer_u, peer_v):
    row = lambda a: a.reshape(1, -1)
    head_ones = jnp.ones((1, HEAD_DIM, HEAD_DIM), F32)
    wts = {
        "g_mix": row(g_mix), "w_in": w_in.astype(BF16), "b_in": row(b_in),
        "g_q": row(jnp.tile(g_q, N_HEADS)), "g_k": row(jnp.tile(g_k, N_HEADS)),
        "ones_bd": _block_diag(jnp.tile(head_ones, (N_HEADS, 1, 1))).astype(BF16),
        "w_conv": w_conv, "b_conv": row(b_conv),
        "wa_bd": _block_diag(w_a).astype(BF16), "b_a": row(b_a),
        "wx_bd": _block_diag(w_x).astype(BF16), "b_x": row(b_x), "lam": row(lam),
        "w_oa": w_oa.astype(BF16), "w_ob": w_ob.astype(BF16), "w_out": w_out.astype(BF16),
        "g_ffn": row(g_ffn), "w_pq_t": w_pq.T.astype(BF16),
        "keys": sub_keys.reshape(PEER_HEADS * 2, PEER_NKEYS, PEER_HALF).astype(BF16),
        "tab_u": _pack_table(peer_u), "tab_v": _pack_table(peer_v),
        "fold": jnp.repeat(jnp.eye(PEER_SEL, dtype=BF16), CHUNKS, axis=0),
    }
    slopes = jnp.asarray(np.array([2.0 ** (-8.0 * (h + 1) / N_HEADS) for h in range(N_HEADS)],
                                  dtype=np.float32))
    bp, tp, _ = x_prompt.shape
    bs, ts, _ = x_sample.shape

    def attend_prompt(q, k, v):
        return _moba_prompt(q, k, v, slopes, bp, tp)

    def attend_sample(q, k, v):
        return _moba_sample(q, k, v, cache_k, cache_v, page_table, slopes, bs, ts)

    conv00 = jnp.zeros((bp, CONV_W - 1, D_LRU), x_prompt.dtype)
    h00 = jnp.zeros((bp, D_LRU), state_h.dtype)
    y_p, k_p, v_p, h_p, c_p = _layer(x_prompt, conv00, h00, attend_prompt, wts, bp, tp)
    y_s, k_s, v_s, h_s, c_s = _layer(x_sample, state_conv, state_h, attend_sample, wts, bs, ts)
    return (y_p, y_s, k_p, v_p, h_p, c_p, k_s, v_s, h_s, c_s)
```
